```python
import jax
import jax.numpy as jnp
from jax import lax
import numpy as np

D_MODEL = 2048
BATCH = 1
SEQ = 16384
DEPTH = 1
DEC_BATCH = 128
DEC_SEQ = 1
PAST_LEN = 16384
PAGE_SIZE = 128

RET_HEADS = 4
RET_DK = 128
RET_DV = 256
RET_CHUNK = 128
ROPE_BASE = 10000.0
ATT_HEADS = 16
ATT_KV_HEADS = 2
ATT_GROUP = ATT_HEADS // ATT_KV_HEADS
ATT_HEAD_DIM = 64
WINDOW = 128
SWA_BLOCK = 128
RET_Q_W = RET_HEADS * RET_DK
RET_V_W = RET_HEADS * RET_DV
ATT_Q_W = ATT_HEADS * ATT_HEAD_DIM
ATT_KV_W = ATT_KV_HEADS * ATT_HEAD_DIM
IN_SPLITS = (RET_Q_W, RET_Q_W, RET_V_W, RET_V_W, ATT_Q_W, ATT_KV_W, ATT_KV_W)
D_IN = 2 * RET_Q_W + 2 * RET_V_W + ATT_Q_W + 2 * ATT_KV_W
D_MIX = RET_V_W + ATT_Q_W
N_EXPERTS = 32
TOP_K = 4
D_FF = D_MODEL
SWIGLU_ALPHA = 1.702
SWIGLU_LIMIT = 7.0
MOE_BLOCK = 128
LN_EPS = 1e-5
GN_EPS = 1e-5
DEEPNORM_ALPHA = (2.0 * DEPTH) ** 0.25
DEEPNORM_BETA = (8.0 * DEPTH) ** -0.25

kernel_name = 'hymba_retention_swa_sink_moe_deepnorm_step'


def _split_points():
    return np.cumsum(np.array(IN_SPLITS))[:-1].tolist()


def layer_norm(x, w, b):
    xf = x.astype(jnp.float32)
    mu = jnp.mean(xf, axis=-1, keepdims=True)
    var = jnp.mean(jnp.square(xf - mu), axis=-1, keepdims=True)
    y = (xf - mu) * lax.rsqrt(var + LN_EPS) * w.astype(jnp.float32) + b.astype(jnp.float32)
    return y.astype(x.dtype)


def rotate(x, pos):
    half = x.shape[-1] // 2
    inv_freq = ROPE_BASE ** (-jnp.arange(half, dtype=jnp.float32) / half)
    ang = pos[:, None] * inv_freq[None, :]
    cos = jnp.cos(ang)[None, :, None, :]
    sin = jnp.sin(ang)[None, :, None, :]
    x1, x2 = x[..., :half], x[..., half:]
    return jnp.concatenate([x1 * cos - x2 * sin, x1 * sin + x2 * cos], axis=-1)


def in_projection(x, w_in, b_in, pos):
    B, L, _ = x.shape
    h = (x @ w_in + b_in).astype(jnp.float32)
    q_r, k_r, v_r, g_r, q_a, k_a, v_a = jnp.split(h, _split_points(), axis=-1)
    q_r = rotate(q_r.reshape(B, L, RET_HEADS, RET_DK), pos)
    k_r = rotate(k_r.reshape(B, L, RET_HEADS, RET_DK), pos) * (RET_DK ** -0.5)
    v_r = v_r.reshape(B, L, RET_HEADS, RET_DV)
    q_a = q_a.reshape(B, L, ATT_KV_HEADS, ATT_GROUP, ATT_HEAD_DIM)
    k_a = k_a.reshape(B, L, ATT_KV_HEADS, ATT_HEAD_DIM)
    v_a = v_a.reshape(B, L, ATT_KV_HEADS, ATT_HEAD_DIM)
    return q_r, k_r, v_r, g_r, q_a, k_a, v_a


def retention_log_decay():
    return jnp.log1p(-jnp.exp2(-5.0 - jnp.arange(RET_HEADS, dtype=jnp.float32)))


def retention_chunk(state, q, k, v, log_g):
    C = q.shape[1]
    idx = jnp.arange(C, dtype=jnp.float32)
    diff = idx[:, None] - idx[None, :]
    causal = diff >= 0
    decay = jnp.where(causal[None], jnp.exp(jnp.where(causal, diff, 0.0)[None] * log_g[:, None, None]), 0.0)
    scores = jnp.einsum('bqhd,bkhd->bhqk', q, k) * decay[None]
    o_inner = jnp.einsum('bhqk,bkhv->bqhv', scores, v)
    xi = jnp.exp((idx + 1.0)[:, None] * log_g[None, :])
    o_cross = jnp.einsum('bqhd,bhdv->bqhv', q, state) * xi[None, :, :, None]
    zeta = jnp.exp((C - 1.0 - idx)[:, None] * log_g[None, :])
    new_state = (jnp.exp(C * log_g)[None, :, None, None] * state
                 + jnp.einsum('bkhd,bkhv->bhdv', k * zeta[None, :, :, None], v))
    return new_state, o_inner + o_cross


def retention(q, k, v, state):
    B, L, H, _ = q.shape
    log_g = retention_log_decay()
    c = min(RET_CHUNK, L)
    n_full = L // c
    lf = n_full * c

    def to_chunks(t):
        return jnp.moveaxis(t[:, :lf].reshape((B, n_full, c) + t.shape[2:]), 1, 0)

    def step(s, xs):
        return retention_chunk(s, xs[0], xs[1], xs[2], log_g)

    state, o = lax.scan(step, state, (to_chunks(q), to_chunks(k), to_chunks(v)))
    o = jnp.moveaxis(o, 0, 1).reshape(B, lf, H, RET_DV)
    if lf < L:
        state, o_rem = retention_chunk(state, q[:, lf:], k[:, lf:], v[:, lf:], log_g)
        o = jnp.concatenate([o, o_rem], axis=1)
    return o, state


def retention_output(o, g, gn_w, gn_b):
    B, L = o.shape[:2]
    mu = jnp.mean(o, axis=-1, keepdims=True)
    var = jnp.mean(jnp.square(o - mu), axis=-1, keepdims=True)
    on = ((o - mu) * lax.rsqrt(var + GN_EPS)).reshape(B, L, RET_V_W)
    on = on * gn_w.astype(jnp.float32) + gn_b.astype(jnp.float32)
    return jax.nn.silu(g) * on


def attention_with_sinks(q, k, v, mask, sinks):
    s = jnp.einsum('bnqkgd,bnskd->bnkgqs', q, k) * (ATT_HEAD_DIM ** -0.5)
    s = jnp.where(mask[None, :, None, None], s, -jnp.inf)
    sink = jnp.broadcast_to(sinks.astype(jnp.float32).reshape(ATT_KV_HEADS, ATT_GROUP, 1, 1), s.shape[:-1] + (1,))
    p = jax.nn.softmax(jnp.concatenate([s, sink], axis=-1), axis=-1)[..., :-1]
    return jnp.einsum('bnkgqs,bnskd->bnqkgd', p, v)


def swa_prompt(q, k, v, sinks):
    B, L = q.shape[:2]
    n = -(-L // SWA_BLOCK)
    lp = n * SWA_BLOCK
    pad = lp - L
    qb = jnp.pad(q, ((0, 0), (0, pad), (0, 0), (0, 0), (0, 0))).reshape(B, n, SWA_BLOCK, ATT_KV_HEADS, ATT_GROUP, ATT_HEAD_DIM)

    def band(t):
        tp = jnp.pad(t, ((0, 0), (SWA_BLOCK, pad), (0, 0), (0, 0))).reshape(B, n + 1, SWA_BLOCK, ATT_KV_HEADS, ATT_HEAD_DIM)
        return jnp.concatenate([tp[:, :-1], tp[:, 1:]], axis=2)

    blocks = jnp.arange(n)[:, None] * SWA_BLOCK
    qpos = blocks + jnp.arange(SWA_BLOCK)[None, :]
    kpos = blocks - SWA_BLOCK + jnp.arange(2 * SWA_BLOCK)[None, :]
    d = qpos[:, :, None] - kpos[:, None, :]
    mask = (d >= 0) & (d < WINDOW) & (kpos[:, None, :] >= 0)
    o = attention_with_sinks(qb, band(k), band(v), mask, sinks)
    return o.reshape(B, lp, ATT_Q_W)[:, :L]


def swa_sample(q, k, v, k_buf, v_buf, sinks):
    B, S = q.shape[:2]
    nb = k_buf.shape[1]
    kk = jnp.concatenate([k_buf.astype(jnp.float32), k], axis=1)
    vv = jnp.concatenate([v_buf.astype(jnp.float32), v], axis=1)
    qpos = nb + jnp.arange(S)
    kpos = jnp.arange(nb + S)
    d = qpos[:, None] - kpos[None, :]
    mask = ((d >= 0) & (d < WINDOW))[None]
    o = attention_with_sinks(q[:, None], kk[:, None], vv[:, None], mask, sinks)
    return o.reshape(B, S, ATT_Q_W), kk[:, S:], vv[:, S:]


def moe(x, w_router, b_router, w1, b1, w2, b2):
    N, D = x.shape
    logits = (x @ w_router + b_router).astype(jnp.float32)
    top_val, top_idx = lax.top_k(logits, TOP_K)
    gates = jax.nn.softmax(top_val, axis=-1)
    nk = N * TOP_K
    blk = min(MOE_BLOCK, max(8, nk // N_EXPERTS))
    n_blocks = -(-(nk + N_EXPERTS * (blk - 1)) // blk)
    cap = n_blocks * blk
    flat_e = top_idx.reshape(nk)
    flat_t = jnp.repeat(jnp.arange(N, dtype=jnp.int32), TOP_K)
    flat_g = gates.reshape(nk)
    order = jnp.argsort(flat_e)
    se, st, sg = flat_e[order], flat_t[order], flat_g[order]
    counts = jnp.bincount(flat_e, length=N_EXPERTS)
    starts = jnp.cumsum(counts) - counts
    padded = (counts + blk - 1) // blk * blk
    pends = jnp.cumsum(padded)
    pstarts = pends - padded
    dest = pstarts[se] + jnp.arange(nk) - starts[se]
    buf_t = jnp.full((cap,), N, jnp.int32).at[dest].set(st)
    buf_g = jnp.zeros((cap,), jnp.float32).at[dest].set(sg)
    blk_e = jnp.minimum(jnp.searchsorted(pends, jnp.arange(n_blocks) * blk, side='right'), N_EXPERTS - 1)
    x_pad = jnp.concatenate([x, jnp.zeros((1, D), x.dtype)], axis=0)

    def expert_block(args):
        e, tok = args
        h = (x_pad[tok] @ w1[e] + b1[e]).astype(jnp.float32)
        glu = jnp.minimum(h[:, :D_FF], SWIGLU_LIMIT)
        lin = jnp.clip(h[:, D_FF:], -SWIGLU_LIMIT, SWIGLU_LIMIT)
        act = glu * jax.nn.sigmoid(SWIGLU_ALPHA * glu) * (lin + 1.0)
        return (act.astype(w2.dtype) @ w2[e] + b2[e]).astype(jnp.float32)

    out = lax.map(expert_block, (blk_e, buf_t.reshape(n_blocks, blk)))
    y = jnp.zeros((N + 1, D), jnp.float32).at[buf_t].add(out.reshape(cap, D) * buf_g[:, None])
    return y[:N].astype(x.dtype)


def trunk_layer(x, pos, ret_state, k_buf, v_buf, w_in, b_in, sinks, gn_w, gn_b, w_out,
                ln1_w, ln1_b, w_router, b_router, w1, b1, w2, b2, ln2_w, ln2_b):
    B, L, D = x.shape
    q_r, k_r, v_r, g_r, q_a, k_a, v_a = in_projection(x, w_in, b_in, pos)
    o_r, ret_new = retention(q_r, k_r, v_r, ret_state.astype(jnp.float32))
    if k_buf is None:
        o_a = swa_prompt(q_a, k_a, v_a, sinks)
        rows = min(WINDOW, L)
        k_new, v_new = k_a[:, L - rows:], v_a[:, L - rows:]
    else:
        o_a, k_new, v_new = swa_sample(q_a, k_a, v_a, k_buf, v_buf, sinks)
    mixed = jnp.concatenate([retention_output(o_r, g_r, gn_w, gn_b), o_a], axis=-1).astype(x.dtype) @ w_out
    h = layer_norm(DEEPNORM_ALPHA * x + mixed, ln1_w, ln1_b)
    f = moe(h.reshape(B * L, D), w_router, b_router, w1, b1, w2, b2).reshape(B, L, D)
    y = layer_norm(DEEPNORM_ALPHA * h + f, ln2_w, ln2_b)
    return y, ret_new.astype(x.dtype), k_new.astype(x.dtype), v_new.astype(x.dtype)


def _normal(key, shape, scale):
    return scale * jax.random.normal(key, shape, jnp.float32)


def setup_inputs(seed: int = 0) -> dict:
    key = jax.random.key(seed)
    ks = jax.random.split(key, 22)
    buf = min(WINDOW, PAST_LEN)
    return {
        'x_prompt': _normal(ks[0], (BATCH, SEQ, D_MODEL), 1.0),
        'x_sample': _normal(ks[1], (DEC_BATCH, DEC_SEQ, D_MODEL), 1.0),
        'state_ret': _normal(ks[2], (DEPTH, DEC_BATCH, RET_HEADS, RET_DK, RET_DV), 0.5),
        'cache_swa_k': _normal(ks[3], (DEPTH, DEC_BATCH, buf, ATT_KV_HEADS, ATT_HEAD_DIM), 1.0),
        'cache_swa_v': _normal(ks[4], (DEPTH, DEC_BATCH, buf, ATT_KV_HEADS, ATT_HEAD_DIM), 1.0),
        'w_in': _normal(ks[5], (DEPTH, D_MODEL, D_IN), D_MODEL ** -0.5),
        'b_in': _normal(ks[6], (DEPTH, D_IN), 0.01),
        'sinks': _normal(ks[7], (DEPTH, ATT_HEADS), 0.5),
        'gn_w': 1.0 + _normal(ks[8], (DEPTH, RET_V_W), 0.02),
        'gn_b': _normal(ks[9], (DEPTH, RET_V_W), 0.02),
        'w_out': _normal(ks[10], (DEPTH, D_MIX, D_MODEL), DEEPNORM_BETA * D_MIX ** -0.5),
        'ln1_w': 1.0 + _normal(ks[11], (DEPTH, D_MODEL), 0.02),
        'ln1_b': _normal(ks[12], (DEPTH, D_MODEL), 0.02),
        'w_router': _normal(ks[13], (DEPTH, D_MODEL, N_EXPERTS), D_MODEL ** -0.5),
        'b_router': _normal(ks[14], (DEPTH, N_EXPERTS), 0.01),
        'w1': _normal(ks[15], (DEPTH, N_EXPERTS, D_MODEL, 2 * D_FF), D_MODEL ** -0.5),
        'b1': _normal(ks[16], (DEPTH, N_EXPERTS, 2 * D_FF), 0.01),
        'w2': _normal(ks[17], (DEPTH, N_EXPERTS, D_FF, D_MODEL), DEEPNORM_BETA * D_FF ** -0.5),
        'b2': _normal(ks[18], (DEPTH, N_EXPERTS, D_MODEL), 0.01),
        'ln2_w': 1.0 + _normal(ks[19], (DEPTH, D_MODEL), 0.02),
        'ln2_b': _normal(ks[20], (DEPTH, D_MODEL), 0.02),
    }


def reference(x_prompt, x_sample, state_ret, cache_swa_k, cache_swa_v, w_in, b_in, sinks, gn_w, gn_b,
              w_out, ln1_w, ln1_b, w_router, b_router, w1, b1, w2, b2, ln2_w, ln2_b):
    B, L, _ = x_prompt.shape
    DB, S, _ = x_sample.shape
    pos_p = jnp.arange(L, dtype=jnp.float32)
    pos_s = PAST_LEN + jnp.arange(S, dtype=jnp.float32)
    y_p, y_s = x_prompt, x_sample
    rp, kp, vp, rs, kss, vss = [], [], [], [], [], []
    for l in range(DEPTH):
        lw = (w_in[l], b_in[l], sinks[l], gn_w[l], gn_b[l], w_out[l], ln1_w[l], ln1_b[l],
              w_router[l], b_router[l], w1[l], b1[l], w2[l], b2[l], ln2_w[l], ln2_b[l])
        s0 = jnp.zeros((B, RET_HEADS, RET_DK, RET_DV), jnp.float32)
        y_p, r_new, k_new, v_new = trunk_layer(y_p, pos_p, s0, None, None, *lw)
        rp.append(r_new)
        kp.append(k_new)
        vp.append(v_new)
        y_s, r_new, k_new, v_new = trunk_layer(y_s, pos_s, state_ret[l], cache_swa_k[l], cache_swa_v[l], *lw)
        rs.append(r_new)
        kss.append(k_new)
        vss.append(v_new)
    return (y_p, y_s, jnp.stack(rp), jnp.stack(kp), jnp.stack(vp), jnp.stack(rs), jnp.stack(kss), jnp.stack(vss))
```

```python
import functools

import jax
import jax.numpy as jnp
import numpy as np
from jax import lax
from jax.experimental import pallas as pl
from jax.experimental.pallas import tpu as pltpu

D_MODEL = 2048
RET_HEADS = 4
RET_DK = 128
RET_DV = 256
ROPE_BASE = 10000.0
ATT_HEADS = 16
ATT_KV_HEADS = 2
ATT_GROUP = ATT_HEADS // ATT_KV_HEADS
ATT_HEAD_DIM = 64
WINDOW = 128
RET_Q_W = RET_HEADS * RET_DK
RET_V_W = RET_HEADS * RET_DV
ATT_Q_W = ATT_HEADS * ATT_HEAD_DIM
ATT_KV_W = ATT_KV_HEADS * ATT_HEAD_DIM
D_IN = 2 * RET_Q_W + 2 * RET_V_W + ATT_Q_W + 2 * ATT_KV_W
N_EXPERTS = 32
TOP_K = 4
D_FF = D_MODEL
SWIGLU_ALPHA = 1.702
SWIGLU_LIMIT = 7.0
LN_EPS = 1e-5
GN_EPS = 1e-5
DEPTH = 1
DEEPNORM_ALPHA = (2.0 * DEPTH) ** 0.25
PAST_LEN = 16384

_OFF_QR = 0
_OFF_KR = _OFF_QR + RET_Q_W
_OFF_VR = _OFF_KR + RET_Q_W
_OFF_G = _OFF_VR + RET_V_W
_OFF_QA = _OFF_G + RET_V_W
_OFF_KA = _OFF_QA + ATT_Q_W
_OFF_VA = _OFF_KA + ATT_KV_W

V7X_VMEM_BYTES = 64 * 1024 * 1024
VMEM_LIMIT = 56 * 1024 * 1024
PROJ_ROWS = 256
RET_CHUNK = 256
SWA_BLOCK = 128
MOE_ROWS = 768
MOE_FF = 256
DISPATCH_CHUNK = 512
COMBINE_ROWS = 256
SAMPLE_BATCH_BLOCK = 8

BF16 = jnp.bfloat16
F32 = jnp.float32


def _params(sem, vmem=VMEM_LIMIT):
    return pltpu.CompilerParams(dimension_semantics=sem, vmem_limit_bytes=vmem)


def _row_tile(n, pref):
    t = min(pref, n)
    assert n % t == 0 and (t % 8 == 0 or t == n), (n, t)
    return t


def _in_proj_kernel(x_ref, w_ref, b_ref, cos_ref, sin_ref,
                    qr_ref, kr_ref, vr_ref, g_ref, qa_ref, ka_ref, va_ref):
    x = x_ref[...].astype(BF16)

    def proj(lo, width):
        return jnp.dot(x, w_ref[:, lo:lo + width], preferred_element_type=F32) + b_ref[:, lo:lo + width]

    cos = cos_ref[...]
    sin = sin_ref[...]

    def rotate_heads(h, out_ref, scale):
        for hd in range(RET_HEADS):
            xh = h[:, hd * RET_DK:(hd + 1) * RET_DK]
            r = xh * cos + pltpu.roll(xh, RET_DK // 2, axis=1) * sin
            if scale != 1.0:
                r = r * scale
            out_ref[:, hd * RET_DK:(hd + 1) * RET_DK] = r

    rotate_heads(proj(_OFF_QR, RET_Q_W), qr_ref, 1.0)
    rotate_heads(proj(_OFF_KR, RET_Q_W), kr_ref, RET_DK ** -0.5)
    vr_ref[...] = proj(_OFF_VR, RET_V_W)
    g_ref[...] = proj(_OFF_G, RET_V_W)
    qa_ref[...] = proj(_OFF_QA, ATT_Q_W)
    kv = proj(_OFF_KA, 2 * ATT_KV_W)
    ka_ref[...] = kv[:, :ATT_KV_W]
    va_ref[...] = kv[:, ATT_KV_W:]


def _in_proj(x, w_bf, b, cos_t, sin_t):
    n = x.shape[0]
    tm = _row_tile(n, PROJ_ROWS)
    row = lambda w: pl.BlockSpec((tm, w), lambda i: (i, 0))
    full = lambda a: pl.BlockSpec(a.shape, lambda i: (0,) * a.ndim)
    widths = (RET_Q_W, RET_Q_W, RET_V_W, RET_V_W, ATT_Q_W, ATT_KV_W, ATT_KV_W)
    return pl.pallas_call(
        _in_proj_kernel,
        grid=(n // tm,),
        in_specs=[row(D_MODEL), full(w_bf), full(b), row(RET_DK), row(RET_DK)],
        out_specs=[row(w) for w in widths],
        out_shape=[jax.ShapeDtypeStruct((n, w), F32) for w in widths],
        compiler_params=_params(("parallel",)),
        name="in_proj",
    )(x, w_bf, b, cos_t, sin_t)


def _rope_tables(pos):
    half = RET_DK // 2
    inv_freq = ROPE_BASE ** (-jnp.arange(half, dtype=F32) / half)
    ang = pos[:, None] * inv_freq[None, :]
    c, s = jnp.cos(ang), jnp.sin(ang)
    return jnp.concatenate([c, c], axis=-1), jnp.concatenate([-s, s], axis=-1)


def _retention_log_decay():
    return jnp.log1p(-jnp.exp2(-5.0 - jnp.arange(RET_HEADS, dtype=F32)))


def _ret_prompt_kernel(q_ref, k_ref, v_ref, dec_ref, xi_ref, zeta_ref, gc_ref, o_ref, st_ref, state):
    c = pl.program_id(1)

    @pl.when(c == 0)
    def _():
        state[...] = jnp.zeros_like(state)

    q = q_ref[...].astype(BF16)
    k = k_ref[...]
    v = v_ref[...].astype(BF16)
    s_old = state[...]
    scores = lax.dot_general(q, k.astype(BF16), (((1,), (1,)), ((), ())), preferred_element_type=F32)
    scores = scores * dec_ref[...]
    o_inner = jnp.dot(scores.astype(BF16), v, preferred_element_type=F32)
    o_cross = jnp.dot(q, s_old.astype(BF16), preferred_element_type=F32) * xi_ref[...]
    o_ref[...] = o_inner + o_cross
    kz = (k * zeta_ref[...]).astype(BF16)
    upd = lax.dot_general(kz, v, (((0,), (0,)), ((), ())), preferred_element_type=F32)
    s_new = gc_ref[...] * s_old + upd
    state[...] = s_new

    @pl.when(c == pl.num_programs(1) - 1)
    def _():
        st_ref[...] = s_new


def _ret_prompt(qr, kr, vr):
    L = qr.shape[0]
    C = _row_tile(L, RET_CHUNK)
    log_g = _retention_log_decay()
    idx = jnp.arange(C, dtype=F32)
    diff = idx[:, None] - idx[None, :]
    causal = diff >= 0
    dec = jnp.where(causal[None], jnp.exp(jnp.where(causal, diff, 0.0)[None] * log_g[:, None, None]), 0.0)
    xi = jnp.exp((idx + 1.0)[None, :, None] * log_g[:, None, None])
    zeta = jnp.exp((C - 1.0 - idx)[None, :, None] * log_g[:, None, None])
    gc = jnp.broadcast_to(jnp.exp(C * log_g)[:, None, None], (RET_HEADS, 1, RET_DV))
    return pl.pallas_call(
        _ret_prompt_kernel,
        grid=(RET_HEADS, L // C),
        in_specs=[
            pl.BlockSpec((C, RET_DK), lambda h, c: (c, h)),
            pl.BlockSpec((C, RET_DK), lambda h, c: (c, h)),
            pl.BlockSpec((C, RET_DV), lambda h, c: (c, h)),
            pl.BlockSpec((None, C, C), lambda h, c: (h, 0, 0)),
            pl.BlockSpec((None, C, 1), lambda h, c: (h, 0, 0)),
            pl.BlockSpec((None, C, 1), lambda h, c: (h, 0, 0)),
            pl.BlockSpec((None, 1, RET_DV), lambda h, c: (h, 0, 0)),
        ],
        out_specs=[
            pl.BlockSpec((C, RET_DV), lambda h, c: (c, h)),
            pl.BlockSpec((None, RET_DK, RET_DV), lambda h, c: (h, 0, 0)),
        ],
        out_shape=[
            jax.ShapeDtypeStruct((L, RET_V_W), F32),
            jax.ShapeDtypeStruct((RET_HEADS, RET_DK, RET_DV), F32),
        ],
        scratch_shapes=[pltpu.VMEM((RET_DK, RET_DV), F32)],
        compiler_params=_params(("arbitrary", "arbitrary")),
        name="retention_prompt",
    )(qr, kr, vr, dec, xi, zeta, gc)


def _ret_sample_kernel(gam_ref, qt_ref, kt_ref, v_ref, s_ref, o_ref, sn_ref):
    bb = v_ref.shape[0]
    for i in range(bb):
        for h in range(RET_HEADS):
            gam = gam_ref[h]
            qc = qt_ref[h, :, i:i + 1]
            kc = kt_ref[h, :, i:i + 1]
            vrow = v_ref[i, h:h + 1, :]
            st = s_ref[i, h]
            qk = jnp.sum(qc * kc, axis=0, keepdims=True)
            cross = jnp.sum(qc * st, axis=0, keepdims=True) * gam
            o_ref[i, h:h + 1, :] = qk * vrow + cross
            sn_ref[i, h] = gam * st + kc * vrow


def _ret_sample(qr, kr, vr, state):
    db = qr.shape[0]
    bb = _row_tile(db, SAMPLE_BATCH_BLOCK)
    nb = db // bb

    def cols(t):
        return t.reshape(nb, bb, RET_HEADS, RET_DK).transpose(0, 2, 3, 1)

    gam = jnp.exp(_retention_log_decay())
    o, s_new = pl.pallas_call(
        _ret_sample_kernel,
        grid_spec=pltpu.PrefetchScalarGridSpec(
            num_scalar_prefetch=1,
            grid=(nb,),
            in_specs=[
                pl.BlockSpec((None, RET_HEADS, RET_DK, bb), lambda b, g: (b, 0, 0, 0)),
                pl.BlockSpec((None, RET_HEADS, RET_DK, bb), lambda b, g: (b, 0, 0, 0)),
                pl.BlockSpec((bb, RET_HEADS, RET_DV), lambda b, g: (b, 0, 0)),
                pl.BlockSpec((bb, RET_HEADS, RET_DK, RET_DV), lambda b, g: (b, 0, 0, 0)),
            ],
            out_specs=[
                pl.BlockSpec((bb, RET_HEADS, RET_DV), lambda b, g: (b, 0, 0)),
                pl.BlockSpec((bb, RET_HEADS, RET_DK, RET_DV), lambda b, g: (b, 0, 0, 0)),
            ],
        ),
        out_shape=[
            jax.ShapeDtypeStruct((db, RET_HEADS, RET_DV), F32),
            jax.ShapeDtypeStruct((db, RET_HEADS, RET_DK, RET_DV), F32),
        ],
        compiler_params=_params(("parallel",)),
        name="retention_sample",
    )(gam, cols(qr), cols(kr), vr.reshape(db, RET_HEADS, RET_DV), state)
    return o.reshape(db, RET_V_W), s_new


def _softmax_sink_pv(s, sink, v_bf):
    m = jnp.maximum(jnp.max(s, axis=-1, keepdims=True), sink)
    p = jnp.exp(s - m)
    denom = jnp.sum(p, axis=-1, keepdims=True) + jnp.exp(sink - m)
    pv = jnp.dot(p.astype(BF16), v_bf, preferred_element_type=F32)
    return pv / denom


def _swa_prompt_kernel(sink_ref, q_ref, kp_ref, kc_ref, vp_ref, vc_ref, o_ref):
    b = pl.program_id(0)
    blk = q_ref.shape[0]
    qi = lax.broadcasted_iota(jnp.int32, (blk, 2 * blk), 0)
    kj = lax.broadcasted_iota(jnp.int32, (blk, 2 * blk), 1)
    valid = (kj > qi + (blk - WINDOW)) & (kj <= qi + blk) & ((kj >= blk) | (b > 0))
    k2 = jnp.concatenate([kp_ref[...], kc_ref[...]], axis=0).astype(BF16)
    v2 = jnp.concatenate([vp_ref[...], vc_ref[...]], axis=0).astype(BF16)
    for hh in range(ATT_HEADS):
        j = hh // ATT_GROUP
        qh = q_ref[:, hh * ATT_HEAD_DIM:(hh + 1) * ATT_HEAD_DIM].astype(BF16)
        kh = k2[:, j * ATT_HEAD_DIM:(j + 1) * ATT_HEAD_DIM]
        vh = v2[:, j * ATT_HEAD_DIM:(j + 1) * ATT_HEAD_DIM]
        s = lax.dot_general(qh, kh, (((1,), (1,)), ((), ())), preferred_element_type=F32)
        s = jnp.where(valid, s * (ATT_HEAD_DIM ** -0.5), -jnp.inf)
        o_ref[:, hh * ATT_HEAD_DIM:(hh + 1) * ATT_HEAD_DIM] = _softmax_sink_pv(s, sink_ref[hh], vh)


def _swa_prompt(sinks, qa, ka, va):
    L = qa.shape[0]
    blk = SWA_BLOCK
    assert L % blk == 0 and blk >= WINDOW
    cur = lambda w: pl.BlockSpec((blk, w), lambda b, s: (b, 0))
    prev = lambda w: pl.BlockSpec((blk, w), lambda b, s: (jnp.maximum(b - 1, 0), 0))
    return pl.pallas_call(
        _swa_prompt_kernel,
        grid_spec=pltpu.PrefetchScalarGridSpec(
            num_scalar_prefetch=1,
            grid=(L // blk,),
            in_specs=[cur(ATT_Q_W), prev(ATT_KV_W), cur(ATT_KV_W), prev(ATT_KV_W), cur(ATT_KV_W)],
            out_specs=cur(ATT_Q_W),
        ),
        out_shape=jax.ShapeDtypeStruct((L, ATT_Q_W), F32),
        compiler_params=_params(("parallel",)),
        name="swa_prompt",
    )(sinks, qa, ka, ka, va, va)


def _swa_sample_kernel(sink_ref, q_ref, kn_ref, vn_ref, kb_ref, vb_ref, o_ref, ko_ref, vo_ref):
    bb, nbuf = kb_ref.shape[0], kb_ref.shape[1]
    for i in range(bb):
        ko_ref[i, 0:nbuf - 1, :] = kb_ref[i, 1:nbuf, :]
        ko_ref[i, nbuf - 1:nbuf, :] = kn_ref[i:i + 1, :]
        vo_ref[i, 0:nbuf - 1, :] = vb_ref[i, 1:nbuf, :]
        vo_ref[i, nbuf - 1:nbuf, :] = vn_ref[i:i + 1, :]
        kk = ko_ref[i].astype(BF16)
        vv = vo_ref[i].astype(BF16)
        q = q_ref[i].astype(BF16)
        for j in range(ATT_KV_HEADS):
            qj = q[j * ATT_GROUP:(j + 1) * ATT_GROUP]
            kj = kk[:, j * ATT_HEAD_DIM:(j + 1) * ATT_HEAD_DIM]
            vj = vv[:, j * ATT_HEAD_DIM:(j + 1) * ATT_HEAD_DIM]
            s = lax.dot_general(qj, kj, (((1,), (1,)), ((), ())), preferred_element_type=F32)
            s = s * (ATT_HEAD_DIM ** -0.5)
            sink = sink_ref[j * ATT_GROUP:(j + 1) * ATT_GROUP, :]
            o_ref[i, j * ATT_GROUP:(j + 1) * ATT_GROUP, :] = _softmax_sink_pv(s, sink, vj)


def _swa_sample(sinks, qa, ka, va, k_buf, v_buf):
    db, nbuf = k_buf.shape[0], k_buf.shape[1]
    assert nbuf == WINDOW, "single-token step with a full window-sized cache"
    bb = _row_tile(db, SAMPLE_BATCH_BLOCK)
    kb = k_buf.reshape(db, nbuf, ATT_KV_W)
    vb = v_buf.reshape(db, nbuf, ATT_KV_W)
    row = lambda w: pl.BlockSpec((bb, w), lambda b: (b, 0))
    cache = pl.BlockSpec((bb, nbuf, ATT_KV_W), lambda b: (b, 0, 0))
    heads = pl.BlockSpec((bb, ATT_HEADS, ATT_HEAD_DIM), lambda b: (b, 0, 0))
    o, ko, vo = pl.pallas_call(
        _swa_sample_kernel,
        grid=(db // bb,),
        in_specs=[pl.BlockSpec((ATT_HEADS, 1), lambda b: (0, 0)), heads, row(ATT_KV_W), row(ATT_KV_W), cache, cache],
        out_specs=[heads, cache, cache],
        out_shape=[
            jax.ShapeDtypeStruct((db, ATT_HEADS, ATT_HEAD_DIM), F32),
            jax.ShapeDtypeStruct((db, nbuf, ATT_KV_W), F32),
            jax.ShapeDtypeStruct((db, nbuf, ATT_KV_W), F32),
        ],
        compiler_params=_params(("parallel",)),
        name="swa_sample",
    )(sinks.reshape(ATT_HEADS, 1), qa.reshape(db, ATT_HEADS, ATT_HEAD_DIM), ka, va, kb, vb)
    shape5 = (db, nbuf, ATT_KV_HEADS, ATT_HEAD_DIM)
    return o.reshape(db, ATT_Q_W), ko.reshape(shape5), vo.reshape(shape5)


def _layer_norm(z, w, b):
    mu = jnp.mean(z, axis=-1, keepdims=True)
    zc = z - mu
    var = jnp.mean(zc * zc, axis=-1, keepdims=True)
    return zc * lax.rsqrt(var + LN_EPS) * w + b


def _mix_kernel(or_ref, g_ref, oa_ref, x_ref, wo_ref, gnw_ref, gnb_ref, lnw_ref, lnb_ref, wr_ref, br_ref,
                h_ref, h3_ref, idx_ref, gate_ref):
    acc = jnp.dot(oa_ref[...].astype(BF16), wo_ref[RET_V_W:, :], preferred_element_type=F32)
    for hd in range(RET_HEADS):
        sl = slice(hd * RET_DV, (hd + 1) * RET_DV)
        o = or_ref[:, sl]
        mu = jnp.mean(o, axis=-1, keepdims=True)
        oc = o - mu
        var = jnp.mean(oc * oc, axis=-1, keepdims=True)
        on = oc * lax.rsqrt(var + GN_EPS) * gnw_ref[:, sl] + gnb_ref[:, sl]
        g = g_ref[:, sl]
        r = g * jax.nn.sigmoid(g) * on
        acc = acc + jnp.dot(r.astype(BF16), wo_ref[sl, :], preferred_element_type=F32)
    h = _layer_norm(DEEPNORM_ALPHA * x_ref[...] + acc, lnw_ref[...], lnb_ref[...])
    h_ref[...] = h
    h3_ref[...] = h.reshape(h3_ref.shape)

    logits = jnp.dot(h, wr_ref[...], preferred_element_type=F32, precision=lax.Precision.HIGHEST) + br_ref[...]
    lane = lax.broadcasted_iota(jnp.int32, logits.shape, 1)
    vals, idxs = [], []
    for _ in range(TOP_K):
        m = jnp.max(logits, axis=-1, keepdims=True)
        sel = jnp.min(jnp.where(logits == m, lane, N_EXPERTS), axis=-1, keepdims=True)
        vals.append(m)
        idxs.append(sel)
        logits = jnp.where(lane == sel, -jnp.inf, logits)
    exps = [jnp.exp(v - vals[0]) for v in vals]
    tot = exps[0] + exps[1] + exps[2] + exps[3]
    for kk in range(TOP_K):
        idx_ref[:, kk:kk + 1] = idxs[kk]
        gate_ref[:, kk:kk + 1] = exps[kk] / tot


def _mix(o_r, g_r, o_a, x, wo_bf, gn_w, gn_b, ln_w, ln_b, w_router, b_router):
    n = x.shape[0]
    tm = _row_tile(n, PROJ_ROWS)
    row = lambda w: pl.BlockSpec((tm, w), lambda i: (i, 0))
    full = lambda a: pl.BlockSpec(a.shape, lambda i: (0,) * a.ndim)
    return pl.pallas_call(
        _mix_kernel,
        grid=(n // tm,),
        in_specs=[row(RET_V_W), row(RET_V_W), row(ATT_Q_W), row(D_MODEL), full(wo_bf), full(gn_w), full(gn_b),
                  full(ln_w), full(ln_b), full(w_router), full(b_router)],
        out_specs=[row(D_MODEL), pl.BlockSpec((tm, 1, D_MODEL), lambda i: (i, 0, 0)), row(TOP_K), row(TOP_K)],
        out_shape=[
            jax.ShapeDtypeStruct((n, D_MODEL), F32),
            jax.ShapeDtypeStruct((n, 1, D_MODEL), F32),
            jax.ShapeDtypeStruct((n, TOP_K), jnp.int32),
            jax.ShapeDtypeStruct((n, TOP_K), F32),
        ],
        compiler_params=_params(("parallel",)),
        name="mix_outproj_ln1_router",
    )(o_r, g_r, o_a, x, wo_bf, gn_w, gn_b, ln_w, ln_b, w_router, b_router)


def _routing_tables(top_idx, tm):
    nk = top_idx.size
    flat_e = top_idx.reshape(nk)
    onehot = (flat_e[:, None] == jnp.arange(N_EXPERTS, dtype=jnp.int32)[None, :]).astype(jnp.int32)
    csum = jnp.cumsum(onehot, axis=0)
    rank = jnp.sum(csum * onehot, axis=1) - 1
    counts = csum[-1]
    items = (counts + tm - 1) // tm
    item_end = jnp.cumsum(items)
    n_used = item_end[-1]
    pos = ((item_end - items) * tm)[flat_e] + rank
    n_items = nk // tm + N_EXPERTS
    w = jnp.minimum(jnp.arange(n_items, dtype=jnp.int32), n_used - 1)
    item_e = jnp.minimum(jnp.searchsorted(item_end, w, side="right"), N_EXPERTS - 1).astype(jnp.int32)
    return pos.astype(jnp.int32), item_e, n_used.astype(jnp.int32).reshape(1), n_items


def _dispatch_kernel(pos_ref, hp_ref, hs_ref, xg_init_ref, xg_ref, sem, *, chunk, n_prompt_chunks, n_prompt):
    del xg_init_ref
    c = pl.program_id(0)
    base = c * chunk

    def copies(src_ref, tok_off):
        def mk(i):
            p = base + i
            return pltpu.make_async_copy(src_ref.at[p // TOP_K - tok_off], xg_ref.at[pos_ref[p]], sem)

        def start(i, carry):
            mk(i).start()
            return carry

        def wait(i, carry):
            mk(i).wait()
            return carry

        lax.fori_loop(0, chunk, start, 0)
        lax.fori_loop(0, chunk, wait, 0)

    @pl.when(c < n_prompt_chunks)
    def _():
        copies(hp_ref, 0)

    @pl.when(c >= n_prompt_chunks)
    def _():
        copies(hs_ref, n_prompt)


def _dispatch(pos, h3_p, h3_s, cap):
    n_p, n_s = h3_p.shape[0], h3_s.shape[0]
    chunk = min(DISPATCH_CHUNK, n_s * TOP_K)
    assert (n_p * TOP_K) % chunk == 0 and (n_s * TOP_K) % chunk == 0
    n_chunks = (n_p + n_s) * TOP_K // chunk
    kern = functools.partial(_dispatch_kernel, chunk=chunk, n_prompt_chunks=n_p * TOP_K // chunk, n_prompt=n_p)
    return pl.pallas_call(
        kern,
        grid_spec=pltpu.PrefetchScalarGridSpec(
            num_scalar_prefetch=1,
            grid=(n_chunks,),
            in_specs=[pl.BlockSpec(memory_space=pl.ANY)] * 3,
            out_specs=pl.BlockSpec(memory_space=pl.ANY),
            scratch_shapes=[pltpu.SemaphoreType.DMA(())],
        ),
        out_shape=jax.ShapeDtypeStruct((cap, 1, D_MODEL), F32),
        input_output_aliases={3: 0},
        compiler_params=pltpu.CompilerParams(dimension_semantics=("arbitrary",), has_side_effects=True),
        name="moe_dispatch",
    )(pos, h3_p, h3_s, jnp.zeros((cap, 1, D_MODEL), F32))


def _expert_kernel(item_e_ref, n_used_ref, x_ref, w1g_ref, w1l_ref, b1g_ref, b1l_ref, w2_ref, b2_ref,
                   o_ref, x_bf, acc):
    w = pl.program_id(0)
    j = pl.program_id(1)
    nj = pl.num_programs(1)

    @pl.when(w < n_used_ref[0])
    def _():
        @pl.when(j == 0)
        def _():
            x_bf[...] = x_ref[...].reshape(x_bf.shape).astype(BF16)

        x = x_bf[...]
        hg = jnp.dot(x, w1g_ref[...].astype(BF16), preferred_element_type=F32) + b1g_ref[...]
        hl = jnp.dot(x, w1l_ref[...].astype(BF16), preferred_element_type=F32) + b1l_ref[...]
        glu = jnp.minimum(hg, SWIGLU_LIMIT)
        lin = jnp.clip(hl, -SWIGLU_LIMIT, SWIGLU_LIMIT)
        act = glu * jax.nn.sigmoid(SWIGLU_ALPHA * glu) * (lin + 1.0)
        part = jnp.dot(act.astype(BF16), w2_ref[...].astype(BF16), preferred_element_type=F32)

        @pl.when(j == 0)
        def _():
            acc[...] = part + b2_ref[...]

        @pl.when(j > 0)
        def _():
            acc[...] += part

        @pl.when(j == nj - 1)
        def _():
            o_ref[...] = acc[...].reshape(o_ref.shape)

    @pl.when((w >= n_used_ref[0]) & (j == nj - 1))
    def _():
        o_ref[...] = jnp.zeros_like(o_ref)


def _experts(item_e, n_used, xg3, w1, b1, w2, b2, n_items, tm):
    tf = MOE_FF
    nj = D_FF // tf

    def item(w, j, ie, nu):
        return jnp.minimum(w, nu[0] - 1)

    def ff(w, j, ie, nu):
        return jnp.where(w < nu[0], j, nj - 1)

    b1r = b1.reshape(N_EXPERTS, 1, 2 * D_FF)
    b2r = b2.reshape(N_EXPERTS, 1, D_MODEL)
    return pl.pallas_call(
        _expert_kernel,
        grid_spec=pltpu.PrefetchScalarGridSpec(
            num_scalar_prefetch=2,
            grid=(n_items, nj),
            in_specs=[
                pl.BlockSpec((tm, 1, D_MODEL), lambda w, j, ie, nu: (item(w, j, ie, nu), 0, 0)),
                pl.BlockSpec((None, D_MODEL, tf), lambda w, j, ie, nu: (ie[w], 0, ff(w, j, ie, nu))),
                pl.BlockSpec((None, D_MODEL, tf), lambda w, j, ie, nu: (ie[w], 0, nj + ff(w, j, ie, nu))),
                pl.BlockSpec((None, 1, tf), lambda w, j, ie, nu: (ie[w], 0, ff(w, j, ie, nu))),
                pl.BlockSpec((None, 1, tf), lambda w, j, ie, nu: (ie[w], 0, nj + ff(w, j, ie, nu))),
                pl.BlockSpec((None, tf, D_MODEL), lambda w, j, ie, nu: (ie[w], ff(w, j, ie, nu), 0)),
                pl.BlockSpec((None, 1, D_MODEL), lambda w, j, ie, nu: (ie[w], 0, 0)),
            ],
            out_specs=pl.BlockSpec((tm, 1, D_MODEL), lambda w, j, ie, nu: (w, 0, 0)),
            scratch_shapes=[pltpu.VMEM((tm, D_MODEL), BF16), pltpu.VMEM((tm, D_MODEL), F32)],
        ),
        out_shape=jax.ShapeDtypeStruct((n_items * tm, 1, D_MODEL), F32),
        compiler_params=_params(("arbitrary", "arbitrary")),
        name="moe_experts",
    )(item_e, n_used, xg3, w1, w1, b1r, b1r, w2, b2r)


def _combine_kernel(pos_ref, gate_ref, h_ref, eo_ref, lnw_ref, lnb_ref, y_ref, buf, sem, *, tc):
    base = pl.program_id(0) * (tc * TOP_K)

    def mk(i):
        t = i // TOP_K
        k = i - t * TOP_K
        return pltpu.make_async_copy(eo_ref.at[pos_ref[base + i]], buf.at[k * tc + t], sem)

    def start(i, carry):
        mk(i).start()
        return carry

    def wait(i, carry):
        mk(i).wait()
        return carry

    lax.fori_loop(0, tc * TOP_K, start, 0)
    lax.fori_loop(0, tc * TOP_K, wait, 0)
    gates = gate_ref[...]
    f = jnp.zeros((tc, D_MODEL), F32)
    for k in range(TOP_K):
        f = f + gates[:, k:k + 1] * buf[k * tc:(k + 1) * tc].reshape(tc, D_MODEL)
    y_ref[...] = _layer_norm(DEEPNORM_ALPHA * h_ref[...] + f, lnw_ref[...], lnb_ref[...])


def _combine(pos, gates, h, eo3, ln_w, ln_b):
    n = h.shape[0]
    tc = _row_tile(n, COMBINE_ROWS)
    full = lambda a: pl.BlockSpec(a.shape, lambda i, p: (0,) * a.ndim)
    return pl.pallas_call(
        functools.partial(_combine_kernel, tc=tc),
        grid_spec=pltpu.PrefetchScalarGridSpec(
            num_scalar_prefetch=1,
            grid=(n // tc,),
            in_specs=[
                pl.BlockSpec((tc, TOP_K), lambda i, p: (i, 0)),
                pl.BlockSpec((tc, D_MODEL), lambda i, p: (i, 0)),
                pl.BlockSpec(memory_space=pl.ANY),
                full(ln_w), full(ln_b),
            ],
            out_specs=pl.BlockSpec((tc, D_MODEL), lambda i, p: (i, 0)),
            scratch_shapes=[pltpu.VMEM((tc * TOP_K, 1, D_MODEL), F32), pltpu.SemaphoreType.DMA(())],
        ),
        out_shape=jax.ShapeDtypeStruct((n, D_MODEL), F32),
        compiler_params=_params(("arbitrary",)),
        name="moe_combine_ln2",
    )(pos, gates, h, eo3, ln_w, ln_b)


def _row2(a):
    return a.reshape(1, -1)


def kernel(x_prompt, x_sample, state_ret, cache_swa_k, cache_swa_v, w_in, b_in, sinks, gn_w, gn_b, w_out,
           ln1_w, ln1_b, w_router, b_router, w1, b1, w2, b2, ln2_w, ln2_b):
    B, L, _ = x_prompt.shape
    DB, S, _ = x_sample.shape
    assert B == 1 and S == 1 and w_in.shape[0] == DEPTH == 1
    xp = x_prompt.reshape(L, D_MODEL)
    xs = x_sample.reshape(DB, D_MODEL)
    w_in_bf = w_in[0].astype(BF16)
    w_out_bf = w_out[0].astype(BF16)
    b_in2 = _row2(b_in[0])

    cos_p, sin_p = _rope_tables(jnp.arange(L, dtype=F32))
    qr, kr, vr, g_r, qa, ka, va = _in_proj(xp, w_in_bf, b_in2, cos_p, sin_p)
    o_r, st_p = _ret_prompt(qr, kr, vr)
    o_a = _swa_prompt(sinks[0], qa, ka, va)
    rows = min(WINDOW, L)
    k_new_p = ka[L - rows:].reshape(1, 1, rows, ATT_KV_HEADS, ATT_HEAD_DIM)
    v_new_p = va[L - rows:].reshape(1, 1, rows, ATT_KV_HEADS, ATT_HEAD_DIM)

    cos_s, sin_s = _rope_tables(jnp.full((DB,), PAST_LEN, dtype=F32))
    qr_s, kr_s, vr_s, g_s, qa_s, ka_s, va_s = _in_proj(xs, w_in_bf, b_in2, cos_s, sin_s)
    o_r_s, st_s = _ret_sample(qr_s, kr_s, vr_s, state_ret[0])
    o_a_s, k_new_s, v_new_s = _swa_sample(sinks[0], qa_s, ka_s, va_s, cache_swa_k[0], cache_swa_v[0])

    mix_w = (w_out_bf, _row2(gn_w[0]), _row2(gn_b[0]), _row2(ln1_w[0]), _row2(ln1_b[0]), w_router[0],
             _row2(b_router[0]))
    h_p, h3_p, idx_p, gate_p = _mix(o_r, g_r, o_a, xp, *mix_w)
    h_s, h3_s, idx_s, gate_s = _mix(o_r_s, g_s, o_a_s, xs, *mix_w)

    top_idx = jnp.concatenate([idx_p, idx_s], axis=0)
    pos, item_e, n_used, n_items = _routing_tables(top_idx, MOE_ROWS)
    xg3 = _dispatch(pos, h3_p, h3_s, n_items * MOE_ROWS)
    eo3 = _experts(item_e, n_used, xg3, w1[0], b1[0], w2[0], b2[0], n_items, MOE_ROWS)
    ln2 = (_row2(ln2_w[0]), _row2(ln2_b[0]))
    y_p = _combine(pos[:L * TOP_K], gate_p, h_p, eo3, *ln2)
    y_s = _combine(pos[L * TOP_K:], gate_s, h_s, eo3, *ln2)

    return (y_p.reshape(1, L, D_MODEL), y_s.reshape(DB, 1, D_MODEL),
            st_p.reshape(1, 1, RET_HEADS, RET_DK, RET_DV), k_new_p, v_new_p,
            st_s[None], k_new_s[None], v_new_s[None])
```

```python
import functools

import jax
import jax.numpy as jnp
import numpy as np
from jax import lax
from jax.experimental import pallas as pl
from jax.experimental.pallas import tpu as pltpu

D_MODEL = 2048
RET_HEADS = 4
RET_DK = 128
RET_DV = 256
ROPE_BASE = 10000.0
ATT_HEADS = 16
ATT_KV_HEADS = 2
ATT_GROUP = ATT_HEADS // ATT_KV_HEADS
ATT_HEAD_DIM = 64
WINDOW = 128
RET_Q_W = RET_HEADS * RET_DK
RET_V_W = RET_HEADS * RET_DV
ATT_Q_W = ATT_HEADS * ATT_HEAD_DIM
ATT_KV_W = ATT_KV_HEADS * ATT_HEAD_DIM
D_IN = 2 * RET_Q_W + 2 * RET_V_W + ATT_Q_W + 2 * ATT_KV_W
N_EXPERTS = 32
TOP_K = 4
D_FF = D_MODEL
SWIGLU_ALPHA = 1.702
SWIGLU_LIMIT = 7.0
LN_EPS = 1e-5
GN_EPS = 1e-5
DEPTH = 1
DEEPNORM_ALPHA = (2.0 * DEPTH) ** 0.25
PAST_LEN = 16384

_OFF_QR = 0
_OFF_KR = _OFF_QR + RET_Q_W
_OFF_VR = _OFF_KR + RET_Q_W
_OFF_G = _OFF_VR + RET_V_W
_OFF_QA = _OFF_G + RET_V_W
_OFF_KA = _OFF_QA + ATT_Q_W
_OFF_VA = _OFF_KA + ATT_KV_W

V7X_VMEM_BYTES = 64 * 1024 * 1024
VMEM_LIMIT = 56 * 1024 * 1024
PROJ_ROWS = 256
RET_CHUNK = 256
SWA_BLOCK = 128
MOE_ROWS = 768
MOE_FF = 256
DISPATCH_CHUNK = 512
COMBINE_ROWS = 256
SAMPLE_BATCH_BLOCK = 8

BF16 = jnp.bfloat16
F32 = jnp.float32


def _params(sem, vmem=VMEM_LIMIT):
    return pltpu.CompilerParams(dimension_semantics=sem, vmem_limit_bytes=vmem)


def _row_tile(n, pref):
    t = min(pref, n)
    assert n % t == 0 and (t % 8 == 0 or t == n), (n, t)
    return t


def _in_proj_kernel(x_ref, w_ref, b_ref, cos_ref, sin_ref,
                    qr_ref, kr_ref, vr_ref, g_ref, qa_ref, ka_ref, va_ref):
    x = x_ref[...].astype(BF16)

    def proj(lo, width):
        return jnp.dot(x, w_ref[:, lo:lo + width], preferred_element_type=F32) + b_ref[:, lo:lo + width]

    cos = cos_ref[...]
    sin = sin_ref[...]

    def rotate_heads(h, out_ref, scale):
        for hd in range(RET_HEADS):
            xh = h[:, hd * RET_DK:(hd + 1) * RET_DK]
            r = xh * cos + pltpu.roll(xh, RET_DK // 2, axis=1) * sin
            if scale != 1.0:
                r = r * scale
            out_ref[:, hd * RET_DK:(hd + 1) * RET_DK] = r

    rotate_heads(proj(_OFF_QR, RET_Q_W), qr_ref, 1.0)
    rotate_heads(proj(_OFF_KR, RET_Q_W), kr_ref, RET_DK ** -0.5)
    vr_ref[...] = proj(_OFF_VR, RET_V_W)
    g_ref[...] = proj(_OFF_G, RET_V_W)
    qa_ref[...] = proj(_OFF_QA, ATT_Q_W)
    kv = proj(_OFF_KA, 2 * ATT_KV_W)
    ka_ref[...] = kv[:, :ATT_KV_W]
    va_ref[...] = kv[:, ATT_KV_W:]


def _in_proj(x, w_bf, b, cos_t, sin_t):
    n = x.shape[0]
    tm = _row_tile(n, PROJ_ROWS)
    row = lambda w: pl.BlockSpec((tm, w), lambda i: (i, 0))
    full = lambda a: pl.BlockSpec(a.shape, lambda i: (0,) * a.ndim)
    widths = (RET_Q_W, RET_Q_W, RET_V_W, RET_V_W, ATT_Q_W, ATT_KV_W, ATT_KV_W)
    return pl.pallas_call(
        _in_proj_kernel,
        grid=(n // tm,),
        in_specs=[row(D_MODEL), full(w_bf), full(b), row(RET_DK), row(RET_DK)],
        out_specs=[row(w) for w in widths],
        out_shape=[jax.ShapeDtypeStruct((n, w), F32) for w in widths],
        compiler_params=_params(("parallel",)),
        name="in_proj",
    )(x, w_bf, b, cos_t, sin_t)


def _rope_tables(pos):
    half = RET_DK // 2
    inv_freq = ROPE_BASE ** (-jnp.arange(half, dtype=F32) / half)
    ang = pos[:, None] * inv_freq[None, :]
    c, s = jnp.cos(ang), jnp.sin(ang)
    return jnp.concatenate([c, c], axis=-1), jnp.concatenate([-s, s], axis=-1)


def _retention_log_decay():
    return jnp.log1p(-jnp.exp2(-5.0 - jnp.arange(RET_HEADS, dtype=F32)))


def _ret_prompt_kernel(q_ref, k_ref, v_ref, dec_ref, xi_ref, zeta_ref, gc_ref, o_ref, st_ref, state):
    c = pl.program_id(1)

    @pl.when(c == 0)
    def _():
        state[...] = jnp.zeros_like(state)

    q = q_ref[...].astype(BF16)
    k = k_ref[...]
    v = v_ref[...].astype(BF16)
    s_old = state[...]
    scores = lax.dot_general(q, k.astype(BF16), (((1,), (1,)), ((), ())), preferred_element_type=F32)
    scores = scores * dec_ref[...]
    o_inner = jnp.dot(scores.astype(BF16), v, preferred_element_type=F32)
    o_cross = jnp.dot(q, s_old.astype(BF16), preferred_element_type=F32) * xi_ref[...]
    o_ref[...] = o_inner + o_cross
    kz = (k * zeta_ref[...]).astype(BF16)
    upd = lax.dot_general(kz, v, (((0,), (0,)), ((), ())), preferred_element_type=F32)
    s_new = gc_ref[...] * s_old + upd
    state[...] = s_new

    @pl.when(c == pl.num_programs(1) - 1)
    def _():
        st_ref[...] = s_new


def _ret_prompt(qr, kr, vr):
    L = qr.shape[0]
    C = _row_tile(L, RET_CHUNK)
    log_g = _retention_log_decay()
    idx = jnp.arange(C, dtype=F32)
    diff = idx[:, None] - idx[None, :]
    causal = diff >= 0
    dec = jnp.where(causal[None], jnp.exp(jnp.where(causal, diff, 0.0)[None] * log_g[:, None, None]), 0.0)
    xi = jnp.exp((idx + 1.0)[None, :, None] * log_g[:, None, None])
    zeta = jnp.exp((C - 1.0 - idx)[None, :, None] * log_g[:, None, None])
    gc = jnp.broadcast_to(jnp.exp(C * log_g)[:, None, None], (RET_HEADS, 1, RET_DV))
    return pl.pallas_call(
        _ret_prompt_kernel,
        grid=(RET_HEADS, L // C),
        in_specs=[
            pl.BlockSpec((C, RET_DK), lambda h, c: (c, h)),
            pl.BlockSpec((C, RET_DK), lambda h, c: (c, h)),
            pl.BlockSpec((C, RET_DV), lambda h, c: (c, h)),
            pl.BlockSpec((None, C, C), lambda h, c: (h, 0, 0)),
            pl.BlockSpec((None, C, 1), lambda h, c: (h, 0, 0)),
            pl.BlockSpec((None, C, 1), lambda h, c: (h, 0, 0)),
            pl.BlockSpec((None, 1, RET_DV), lambda h, c: (h, 0, 0)),
        ],
        out_specs=[
            pl.BlockSpec((C, RET_DV), lambda h, c: (c, h)),
            pl.BlockSpec((None, RET_DK, RET_DV), lambda h, c: (h, 0, 0)),
        ],
        out_shape=[
            jax.ShapeDtypeStruct((L, RET_V_W), F32),
            jax.ShapeDtypeStruct((RET_HEADS, RET_DK, RET_DV), F32),
        ],
        scratch_shapes=[pltpu.VMEM((RET_DK, RET_DV), F32)],
        compiler_params=_params(("arbitrary", "arbitrary")),
        name="retention_prompt",
    )(qr, kr, vr, dec, xi, zeta, gc)


def _ret_sample_kernel(gam_ref, qt_ref, kt_ref, v_ref, s_ref, o_ref, sn_ref):
    bb = v_ref.shape[0]
    for i in range(bb):
        for h in range(RET_HEADS):
            gam = gam_ref[h]
            qc = qt_ref[h, :, i:i + 1]
            kc = kt_ref[h, :, i:i + 1]
            vrow = v_ref[i, h:h + 1, :]
            st = s_ref[i, h]
            qk = jnp.sum(qc * kc, axis=0, keepdims=True)
            cross = jnp.sum(qc * st, axis=0, keepdims=True) * gam
            o_ref[i, h:h + 1, :] = qk * vrow + cross
            sn_ref[i, h] = gam * st + kc * vrow


def _ret_sample(qr, kr, vr, state):
    db = qr.shape[0]
    bb = _row_tile(db, SAMPLE_BATCH_BLOCK)
    nb = db // bb

    def cols(t):
        return t.reshape(nb, bb, RET_HEADS, RET_DK).transpose(0, 2, 3, 1)

    gam = jnp.exp(_retention_log_decay())
    o, s_new = pl.pallas_call(
        _ret_sample_kernel,
        grid_spec=pltpu.PrefetchScalarGridSpec(
            num_scalar_prefetch=1,
            grid=(nb,),
            in_specs=[
                pl.BlockSpec((None, RET_HEADS, RET_DK, bb), lambda b, g: (b, 0, 0, 0)),
                pl.BlockSpec((None, RET_HEADS, RET_DK, bb), lambda b, g: (b, 0, 0, 0)),
                pl.BlockSpec((bb, RET_HEADS, RET_DV), lambda b, g: (b, 0, 0)),
                pl.BlockSpec((bb, RET_HEADS, RET_DK, RET_DV), lambda b, g: (b, 0, 0, 0)),
            ],
            out_specs=[
                pl.BlockSpec((bb, RET_HEADS, RET_DV), lambda b, g: (b, 0, 0)),
                pl.BlockSpec((bb, RET_HEADS, RET_DK, RET_DV), lambda b, g: (b, 0, 0, 0)),
            ],
        ),
        out_shape=[
            jax.ShapeDtypeStruct((db, RET_HEADS, RET_DV), F32),
            jax.ShapeDtypeStruct((db, RET_HEADS, RET_DK, RET_DV), F32),
        ],
        compiler_params=_params(("parallel",)),
        name="retention_sample",
    )(gam, cols(qr), cols(kr), vr.reshape(db, RET_HEADS, RET_DV), state)
    return o.reshape(db, RET_V_W), s_new


def _softmax_sink_pv(s, sink, v_bf):
    m = jnp.maximum(jnp.max(s, axis=-1, keepdims=True), sink)
    p = jnp.exp(s - m)
    denom = jnp.sum(p, axis=-1, keepdims=True) + jnp.exp(sink - m)
    pv = jnp.dot(p.astype(BF16), v_bf, preferred_element_type=F32)
    return pv / denom


def _swa_prompt_kernel(sink_ref, q_ref, kp_ref, kc_ref, vp_ref, vc_ref, o_ref):
    b = pl.program_id(0)
    blk = q_ref.shape[0]
    qi = lax.broadcasted_iota(jnp.int32, (blk, 2 * blk), 0)
    kj = lax.broadcasted_iota(jnp.int32, (blk, 2 * blk), 1)
    valid = (kj > qi + (blk - WINDOW)) & (kj <= qi + blk) & ((kj >= blk) | (b > 0))
    k2 = jnp.concatenate([kp_ref[...], kc_ref[...]], axis=0).astype(BF16)
    v2 = jnp.concatenate([vp_ref[...], vc_ref[...]], axis=0).astype(BF16)
    for hh in range(ATT_HEADS):
        j = hh // ATT_GROUP
        qh = q_ref[:, hh * ATT_HEAD_DIM:(hh + 1) * ATT_HEAD_DIM].astype(BF16)
        kh = k2[:, j * ATT_HEAD_DIM:(j + 1) * ATT_HEAD_DIM]
        vh = v2[:, j * ATT_HEAD_DIM:(j + 1) * ATT_HEAD_DIM]
        s = lax.dot_general(qh, kh, (((1,), (1,)), ((), ())), preferred_element_type=F32)
        s = jnp.where(valid, s * (ATT_HEAD_DIM ** -0.5), -jnp.inf)
        o_ref[:, hh * ATT_HEAD_DIM:(hh + 1) * ATT_HEAD_DIM] = _softmax_sink_pv(s, sink_ref[hh], vh)


def _swa_prompt(sinks, qa, ka, va):
    L = qa.shape[0]
    blk = SWA_BLOCK
    assert L % blk == 0 and blk >= WINDOW
    cur = lambda w: pl.BlockSpec((blk, w), lambda b, s: (b, 0))
    prev = lambda w: pl.BlockSpec((blk, w), lambda b, s: (jnp.maximum(b - 1, 0), 0))
    return pl.pallas_call(
        _swa_prompt_kernel,
        grid_spec=pltpu.PrefetchScalarGridSpec(
            num_scalar_prefetch=1,
            grid=(L // blk,),
            in_specs=[cur(ATT_Q_W), prev(ATT_KV_W), cur(ATT_KV_W), prev(ATT_KV_W), cur(ATT_KV_W)],
            out_specs=cur(ATT_Q_W),
        ),
        out_shape=jax.ShapeDtypeStruct((L, ATT_Q_W), F32),
        compiler_params=_params(("parallel",)),
        name="swa_prompt",
    )(sinks, qa, ka, ka, va, va)


def _swa_sample_kernel(sink_ref, q_ref, kn_ref, vn_ref, kb_ref, vb_ref, o_ref, ko_ref, vo_ref):
    bb, nbuf = kb_ref.shape[0], kb_ref.shape[1]
    for i in range(bb):
        ko_ref[i, 0:nbuf - 1, :] = kb_ref[i, 1:nbuf, :]
        ko_ref[i, nbuf - 1:nbuf, :] = kn_ref[i:i + 1, :]
        vo_ref[i, 0:nbuf - 1, :] = vb_ref[i, 1:nbuf, :]
        vo_ref[i, nbuf - 1:nbuf, :] = vn_ref[i:i + 1, :]
        kk = ko_ref[i].astype(BF16)
        vv = vo_ref[i].astype(BF16)
        q = q_ref[i].astype(BF16)
        for j in range(ATT_KV_HEADS):
            qj = q[j * ATT_GROUP:(j + 1) * ATT_GROUP]
            kj = kk[:, j * ATT_HEAD_DIM:(j + 1) * ATT_HEAD_DIM]
            vj = vv[:, j * ATT_HEAD_DIM:(j + 1) * ATT_HEAD_DIM]
            s = lax.dot_general(qj, kj, (((1,), (1,)), ((), ())), preferred_element_type=F32)
            s = s * (ATT_HEAD_DIM ** -0.5)
            sink = sink_ref[j * ATT_GROUP:(j + 1) * ATT_GROUP, :]
            o_ref[i, j * ATT_GROUP:(j + 1) * ATT_GROUP, :] = _softmax_sink_pv(s, sink, vj)


def _swa_sample(sinks, qa, ka, va, k_buf, v_buf):
    db, nbuf = k_buf.shape[0], k_buf.shape[1]
    assert nbuf == WINDOW, "single-token step with a full window-sized cache"
    bb = _row_tile(db, SAMPLE_BATCH_BLOCK)
    kb = k_buf.reshape(db, nbuf, ATT_KV_W)
    vb = v_buf.reshape(db, nbuf, ATT_KV_W)
    row = lambda w: pl.BlockSpec((bb, w), lambda b: (b, 0))
    cache = pl.BlockSpec((bb, nbuf, ATT_KV_W), lambda b: (b, 0, 0))
    heads = pl.BlockSpec((bb, ATT_HEADS, ATT_HEAD_DIM), lambda b: (b, 0, 0))
    o, ko, vo = pl.pallas_call(
        _swa_sample_kernel,
        grid=(db // bb,),
        in_specs=[pl.BlockSpec((ATT_HEADS, 1), lambda b: (0, 0)), heads, row(ATT_KV_W), row(ATT_KV_W), cache, cache],
        out_specs=[heads, cache, cache],
        out_shape=[
            jax.ShapeDtypeStruct((db, ATT_HEADS, ATT_HEAD_DIM), F32),
            jax.ShapeDtypeStruct((db, nbuf, ATT_KV_W), F32),
            jax.ShapeDtypeStruct((db, nbuf, ATT_KV_W), F32),
        ],
        compiler_params=_params(("parallel",)),
        name="swa_sample",
    )(sinks.reshape(ATT_HEADS, 1), qa.reshape(db, ATT_HEADS, ATT_HEAD_DIM), ka, va, kb, vb)
    shape5 = (db, nbuf, ATT_KV_HEADS, ATT_HEAD_DIM)
    return o.reshape(db, ATT_Q_W), ko.reshape(shape5), vo.reshape(shape5)


def _layer_norm(z, w, b):
    mu = jnp.mean(z, axis=-1, keepdims=True)
    zc = z - mu
    var = jnp.mean(zc * zc, axis=-1, keepdims=True)
    return zc * lax.rsqrt(var + LN_EPS) * w + b


def _mix_kernel(or_ref, g_ref, oa_ref, x_ref, wo_ref, gnw_ref, gnb_ref, lnw_ref, lnb_ref, wr_ref, br_ref,
                h_ref, h3_ref, idx_ref, gate_ref):
    acc = jnp.dot(oa_ref[...].astype(BF16), wo_ref[RET_V_W:, :], preferred_element_type=F32)
    for hd in range(RET_HEADS):
        sl = slice(hd * RET_DV, (hd + 1) * RET_DV)
        o = or_ref[:, sl]
        mu = jnp.mean(o, axis=-1, keepdims=True)
        oc = o - mu
        var = jnp.mean(oc * oc, axis=-1, keepdims=True)
        on = oc * lax.rsqrt(var + GN_EPS) * gnw_ref[:, sl] + gnb_ref[:, sl]
        g = g_ref[:, sl]
        r = g * jax.nn.sigmoid(g) * on
        acc = acc + jnp.dot(r.astype(BF16), wo_ref[sl, :], preferred_element_type=F32)
    h = _layer_norm(DEEPNORM_ALPHA * x_ref[...] + acc, lnw_ref[...], lnb_ref[...])
    h_ref[...] = h
    h3_ref[...] = h.reshape(h3_ref.shape)

    logits = jnp.dot(h, wr_ref[...], preferred_element_type=F32, precision=lax.Precision.HIGHEST) + br_ref[...]
    lane = lax.broadcasted_iota(jnp.int32, logits.shape, 1)
    vals, idxs = [], []
    for _ in range(TOP_K):
        m = jnp.max(logits, axis=-1, keepdims=True)
        sel = jnp.min(jnp.where(logits == m, lane, N_EXPERTS), axis=-1, keepdims=True)
        vals.append(m)
        idxs.append(sel)
        logits = jnp.where(lane == sel, -jnp.inf, logits)
    exps = [jnp.exp(v - vals[0]) for v in vals]
    tot = exps[0] + exps[1] + exps[2] + exps[3]
    for kk in range(TOP_K):
        idx_ref[:, kk:kk + 1] = idxs[kk]
        gate_ref[:, kk:kk + 1] = exps[kk] / tot


def _mix(o_r, g_r, o_a, x, wo_bf, gn_w, gn_b, ln_w, ln_b, w_router, b_router):
    n = x.shape[0]
    tm = _row_tile(n, PROJ_ROWS)
    row = lambda w: pl.BlockSpec((tm, w), lambda i: (i, 0))
    full = lambda a: pl.BlockSpec(a.shape, lambda i: (0,) * a.ndim)
    return pl.pallas_call(
        _mix_kernel,
        grid=(n // tm,),
        in_specs=[row(RET_V_W), row(RET_V_W), row(ATT_Q_W), row(D_MODEL), full(wo_bf), full(gn_w), full(gn_b),
                  full(ln_w), full(ln_b), full(w_router), full(b_router)],
        out_specs=[row(D_MODEL), pl.BlockSpec((tm, 1, D_MODEL), lambda i: (i, 0, 0)), row(TOP_K), row(TOP_K)],
        out_shape=[
            jax.ShapeDtypeStruct((n, D_MODEL), F32),
            jax.ShapeDtypeStruct((n, 1, D_MODEL), F32),
            jax.ShapeDtypeStruct((n, TOP_K), jnp.int32),
            jax.ShapeDtypeStruct((n, TOP_K), F32),
        ],
        compiler_params=_params(("parallel",)),
        name="mix_outproj_ln1_router",
    )(o_r, g_r, o_a, x, wo_bf, gn_w, gn_b, ln_w, ln_b, w_router, b_router)


def _routing_tables(top_idx, tm, n_prompt):
    nk = top_idx.size
    flat_e = top_idx.reshape(nk)
    onehot = (flat_e[:, None] == jnp.arange(N_EXPERTS, dtype=jnp.int32)[None, :]).astype(jnp.int32)
    csum = jnp.cumsum(onehot, axis=0)
    rank = jnp.sum(csum * onehot, axis=1) - 1
    counts = csum[-1]
    items = (counts + tm - 1) // tm
    item_end = jnp.cumsum(items)
    n_used = item_end[-1]
    item_start = item_end - items
    pos = (item_start * tm)[flat_e] + rank
    n_items = nk // tm + N_EXPERTS
    w_all = jnp.arange(n_items, dtype=jnp.int32)
    w = jnp.minimum(w_all, n_used - 1)
    item_e = jnp.minimum(jnp.sum((item_end[None, :] <= w[:, None]).astype(jnp.int32), axis=1), N_EXPERTS - 1)

    order = jnp.argsort(flat_e, stable=True).astype(jnp.int32)
    starts = jnp.cumsum(counts) - counts
    local = (w - item_start[item_e])[:, None] * tm + jnp.arange(tm, dtype=jnp.int32)[None, :]
    valid = (local < counts[item_e][:, None]) & (w_all < n_used)[:, None]
    sorted_idx = jnp.clip(starts[item_e][:, None] + local, 0, nk - 1)
    src = jnp.where(valid, order[sorted_idx] // TOP_K, n_prompt).astype(jnp.int32)
    split = jnp.sum((src < n_prompt).astype(jnp.int32), axis=1)
    return (pos.astype(jnp.int32), item_e.astype(jnp.int32), n_used.astype(jnp.int32).reshape(1), n_items,
            src.reshape(n_items * tm), split.astype(jnp.int32))


def _dispatch_kernel(src_ref, split_ref, n_used_ref, hp_ref, hs_ref, xg_ref, buf, sem, *, tm, n_prompt):
    w = pl.program_id(0)
    base = w * tm

    @pl.when(w < n_used_ref[0])
    def _():
        def mk_p(i):
            return pltpu.make_async_copy(hp_ref.at[src_ref[base + i]], buf.at[pl.ds(i, 1), :], sem)

        def mk_s(i):
            return pltpu.make_async_copy(hs_ref.at[src_ref[base + i] - n_prompt], buf.at[pl.ds(i, 1), :], sem)

        def loop(mk, lo, hi, wait):
            def body(i, carry):
                if wait:
                    mk(i).wait()
                else:
                    mk(i).start()
                return carry

            lax.fori_loop(lo, hi, body, 0)

        split = split_ref[w]
        loop(mk_p, 0, split, False)
        loop(mk_s, split, tm, False)
        loop(mk_p, 0, split, True)
        loop(mk_s, split, tm, True)
        xg_ref[...] = buf[...].astype(BF16)

    @pl.when(w >= n_used_ref[0])
    def _():
        xg_ref[...] = jnp.zeros_like(xg_ref)


def _dispatch(src, split, n_used, h3_p, h3_s, n_items, tm):
    kern = functools.partial(_dispatch_kernel, tm=tm, n_prompt=h3_p.shape[0])
    return pl.pallas_call(
        kern,
        grid_spec=pltpu.PrefetchScalarGridSpec(
            num_scalar_prefetch=3,
            grid=(n_items,),
            in_specs=[pl.BlockSpec(memory_space=pl.ANY)] * 2,
            out_specs=pl.BlockSpec((tm, D_MODEL), lambda w, *_: (w, 0)),
            scratch_shapes=[pltpu.VMEM((tm, D_MODEL), F32), pltpu.SemaphoreType.DMA(())],
        ),
        out_shape=jax.ShapeDtypeStruct((n_items * tm, D_MODEL), BF16),
        compiler_params=_params(("arbitrary",)),
        name="moe_dispatch",
    )(src, split, n_used, h3_p, h3_s)


def _expert_kernel(item_e_ref, n_used_ref, x_ref, w1g_ref, w1l_ref, b1g_ref, b1l_ref, w2_ref, b2_ref,
                   o_ref, act, *, nj):
    w = pl.program_id(0)
    s = pl.program_id(1)
    used = w < n_used_ref[0]

    @pl.when(used & (s < nj))
    def _():
        x = x_ref[...]
        hg = jnp.dot(x, w1g_ref[...].astype(BF16), preferred_element_type=F32) + b1g_ref[...]
        hl = jnp.dot(x, w1l_ref[...].astype(BF16), preferred_element_type=F32) + b1l_ref[...]
        glu = jnp.minimum(hg, SWIGLU_LIMIT)
        lin = jnp.clip(hl, -SWIGLU_LIMIT, SWIGLU_LIMIT)
        act[s] = (glu * jax.nn.sigmoid(SWIGLU_ALPHA * glu) * (lin + 1.0)).astype(BF16)

    @pl.when(used & (s >= nj))
    def _():
        a = jnp.concatenate([act[jj] for jj in range(nj)], axis=1)
        res = jnp.dot(a, w2_ref[...].astype(BF16), preferred_element_type=F32) + b2_ref[...]
        o_ref[...] = res.reshape(o_ref.shape)

    @pl.when(jnp.logical_not(used) & (s >= nj))
    def _():
        o_ref[...] = jnp.zeros_like(o_ref)


def _experts(item_e, n_used, xg, w1, b1, w2, b2, n_items, tm):
    tf = MOE_FF
    nj = D_FF // tf
    nn = D_MODEL // tf

    def item(w, nu):
        return jnp.minimum(w, nu[0] - 1)

    def ff(w, s, nu):
        return jnp.where(w < nu[0], jnp.minimum(s, nj - 1), nj - 1)

    def col(w, s, nu):
        return jnp.where(w < nu[0], jnp.maximum(s - nj, 0), nn - 1)

    b1r = b1.reshape(N_EXPERTS, 1, 2 * D_FF)
    b2r = b2.reshape(N_EXPERTS, 1, D_MODEL)
    return pl.pallas_call(
        functools.partial(_expert_kernel, nj=nj),
        grid_spec=pltpu.PrefetchScalarGridSpec(
            num_scalar_prefetch=2,
            grid=(n_items, nj + nn),
            in_specs=[
                pl.BlockSpec((tm, D_MODEL), lambda w, s, ie, nu: (item(w, nu), 0)),
                pl.BlockSpec((None, D_MODEL, tf), lambda w, s, ie, nu: (ie[w], 0, ff(w, s, nu))),
                pl.BlockSpec((None, D_MODEL, tf), lambda w, s, ie, nu: (ie[w], 0, nj + ff(w, s, nu))),
                pl.BlockSpec((None, 1, tf), lambda w, s, ie, nu: (ie[w], 0, ff(w, s, nu))),
                pl.BlockSpec((None, 1, tf), lambda w, s, ie, nu: (ie[w], 0, nj + ff(w, s, nu))),
                pl.BlockSpec((None, D_FF, tf), lambda w, s, ie, nu: (ie[w], 0, col(w, s, nu))),
                pl.BlockSpec((None, 1, tf), lambda w, s, ie, nu: (ie[w], 0, col(w, s, nu))),
            ],
            out_specs=pl.BlockSpec((tm, 1, tf), lambda w, s, ie, nu: (w, 0, jnp.maximum(s - nj, 0))),
            scratch_shapes=[pltpu.VMEM((nj, tm, tf), BF16)],
        ),
        out_shape=jax.ShapeDtypeStruct((n_items * tm, 1, D_MODEL), F32),
        compiler_params=_params(("arbitrary", "arbitrary")),
        name="moe_experts",
    )(item_e, n_used, xg, w1, w1, b1r, b1r, w2, b2r)


def _combine_kernel(pos_ref, gate_ref, h_ref, eo_ref, lnw_ref, lnb_ref, y_ref, buf, sem, *, tc):
    base = pl.program_id(0) * (tc * TOP_K)

    def mk(i):
        t = i // TOP_K
        k = i - t * TOP_K
        return pltpu.make_async_copy(eo_ref.at[pos_ref[base + i]], buf.at[pl.ds(k * tc + t, 1), :], sem)

    def start(i, carry):
        mk(i).start()
        return carry

    def wait(i, carry):
        mk(i).wait()
        return carry

    lax.fori_loop(0, tc * TOP_K, start, 0)
    lax.fori_loop(0, tc * TOP_K, wait, 0)
    gates = gate_ref[...]
    f = jnp.zeros((tc, D_MODEL), F32)
    for k in range(TOP_K):
        f = f + gates[:, k:k + 1] * buf[k * tc:(k + 1) * tc, :]
    y_ref[...] = _layer_norm(DEEPNORM_ALPHA * h_ref[...] + f, lnw_ref[...], lnb_ref[...])


def _combine(pos, gates, h, eo3, ln_w, ln_b):
    n = h.shape[0]
    tc = _row_tile(n, COMBINE_ROWS)
    full = lambda a: pl.BlockSpec(a.shape, lambda i, p: (0,) * a.ndim)
    return pl.pallas_call(
        functools.partial(_combine_kernel, tc=tc),
        grid_spec=pltpu.PrefetchScalarGridSpec(
            num_scalar_prefetch=1,
            grid=(n // tc,),
            in_specs=[
                pl.BlockSpec((tc, TOP_K), lambda i, p: (i, 0)),
                pl.BlockSpec((tc, D_MODEL), lambda i, p: (i, 0)),
                pl.BlockSpec(memory_space=pl.ANY),
                full(ln_w), full(ln_b),
            ],
            out_specs=pl.BlockSpec((tc, D_MODEL), lambda i, p: (i, 0)),
            scratch_shapes=[pltpu.VMEM((tc * TOP_K, D_MODEL), F32), pltpu.SemaphoreType.DMA(())],
        ),
        out_shape=jax.ShapeDtypeStruct((n, D_MODEL), F32),
        compiler_params=_params(("arbitrary",)),
        name="moe_combine_ln2",
    )(pos, gates, h, eo3, ln_w, ln_b)


def _row2(a):
    return a.reshape(1, -1)


def kernel(x_prompt, x_sample, state_ret, cache_swa_k, cache_swa_v, w_in, b_in, sinks, gn_w, gn_b, w_out,
           ln1_w, ln1_b, w_router, b_router, w1, b1, w2, b2, ln2_w, ln2_b):
    B, L, _ = x_prompt.shape
    DB, S, _ = x_sample.shape
    assert B == 1 and S == 1 and w_in.shape[0] == DEPTH == 1
    xp = x_prompt.reshape(L, D_MODEL)
    xs = x_sample.reshape(DB, D_MODEL)
    w_in_bf = w_in[0].astype(BF16)
    w_out_bf = w_out[0].astype(BF16)
    b_in2 = _row2(b_in[0])

    cos_p, sin_p = _rope_tables(jnp.arange(L, dtype=F32))
    qr, kr, vr, g_r, qa, ka, va = _in_proj(xp, w_in_bf, b_in2, cos_p, sin_p)
    o_r, st_p = _ret_prompt(qr, kr, vr)
    o_a = _swa_prompt(sinks[0], qa, ka, va)
    rows = min(WINDOW, L)
    k_new_p = ka[L - rows:].reshape(1, 1, rows, ATT_KV_HEADS, ATT_HEAD_DIM)
    v_new_p = va[L - rows:].reshape(1, 1, rows, ATT_KV_HEADS, ATT_HEAD_DIM)

    cos_s, sin_s = _rope_tables(jnp.full((DB,), PAST_LEN, dtype=F32))
    qr_s, kr_s, vr_s, g_s, qa_s, ka_s, va_s = _in_proj(xs, w_in_bf, b_in2, cos_s, sin_s)
    o_r_s, st_s = _ret_sample(qr_s, kr_s, vr_s, state_ret[0])
    o_a_s, k_new_s, v_new_s = _swa_sample(sinks[0], qa_s, ka_s, va_s, cache_swa_k[0], cache_swa_v[0])

    mix_w = (w_out_bf, _row2(gn_w[0]), _row2(gn_b[0]), _row2(ln1_w[0]), _row2(ln1_b[0]), w_router[0],
             _row2(b_router[0]))
    h_p, h3_p, idx_p, gate_p = _mix(o_r, g_r, o_a, xp, *mix_w)
    h_s, h3_s, idx_s, gate_s = _mix(o_r_s, g_s, o_a_s, xs, *mix_w)

    top_idx = jnp.concatenate([idx_p, idx_s], axis=0)
    pos, item_e, n_used, n_items, src, split = _routing_tables(top_idx, MOE_ROWS, L)
    xg = _dispatch(src, split, n_used, h3_p, h3_s, n_items, MOE_ROWS)
    eo3 = _experts(item_e, n_used, xg, w1[0], b1[0], w2[0], b2[0], n_items, MOE_ROWS)
    ln2 = (_row2(ln2_w[0]), _row2(ln2_b[0]))
    y_p = _combine(pos[:L * TOP_K], gate_p, h_p, eo3, *ln2)
    y_s = _combine(pos[L * TOP_K:], gate_s, h_s, eo3, *ln2)

    return (y_p.reshape(1, L, D_MODEL), y_s.reshape(DB, 1, D_MODEL),
            st_p.reshape(1, 1, RET_HEADS, RET_DK, RET_DV), k_new_p, v_new_p,
            st_s[None], k_new_s[None], v_new_s[None])
```

```python
import functools

import jax
import jax.numpy as jnp
import numpy as np
from jax import lax
from jax.experimental import pallas as pl
from jax.experimental.pallas import tpu as pltpu

D_MODEL = 2048
RET_HEADS = 4
RET_DK = 128
RET_DV = 256
ROPE_BASE = 10000.0
ATT_HEADS = 16
ATT_KV_HEADS = 2
ATT_GROUP = ATT_HEADS // ATT_KV_HEADS
ATT_HEAD_DIM = 64
WINDOW = 128
RET_Q_W = RET_HEADS * RET_DK
RET_V_W = RET_HEADS * RET_DV
ATT_Q_W = ATT_HEADS * ATT_HEAD_DIM
ATT_KV_W = ATT_KV_HEADS * ATT_HEAD_DIM
D_IN = 2 * RET_Q_W + 2 * RET_V_W + ATT_Q_W + 2 * ATT_KV_W
N_EXPERTS = 32
TOP_K = 4
D_FF = D_MODEL
SWIGLU_ALPHA = 1.702
SWIGLU_LIMIT = 7.0
LN_EPS = 1e-5
GN_EPS = 1e-5
DEPTH = 1
DEEPNORM_ALPHA = (2.0 * DEPTH) ** 0.25
PAST_LEN = 16384

_OFF_QR = 0
_OFF_KR = _OFF_QR + RET_Q_W
_OFF_VR = _OFF_KR + RET_Q_W
_OFF_G = _OFF_VR + RET_V_W
_OFF_QA = _OFF_G + RET_V_W
_OFF_KA = _OFF_QA + ATT_Q_W
_OFF_VA = _OFF_KA + ATT_KV_W

V7X_VMEM_BYTES = 64 * 1024 * 1024
VMEM_LIMIT = 56 * 1024 * 1024
PROJ_ROWS = 256
RET_CHUNK = 256
SWA_BLOCK = 128
MOE_SUB = 768
MOE_SUBS_PER_ITEM = 3
MOE_ROWS = MOE_SUB * MOE_SUBS_PER_ITEM
MOE_FF = 256
COMBINE_ROWS = 256
DMA_UNROLL = 8
SAMPLE_BATCH_BLOCK = 8

BF16 = jnp.bfloat16
F32 = jnp.float32


def _params(sem, vmem=VMEM_LIMIT):
    return pltpu.CompilerParams(dimension_semantics=sem, vmem_limit_bytes=vmem)


def _row_tile(n, pref):
    t = min(pref, n)
    assert n % t == 0 and (t % 8 == 0 or t == n), (n, t)
    return t


def _in_proj_kernel(x_ref, w_ref, b_ref, cos_ref, sin_ref,
                    qr_ref, kr_ref, vr_ref, g_ref, qa_ref, ka_ref, va_ref):
    x = x_ref[...].astype(BF16)

    def proj(lo, width):
        return jnp.dot(x, w_ref[:, lo:lo + width], preferred_element_type=F32) + b_ref[:, lo:lo + width]

    cos = cos_ref[...]
    sin = sin_ref[...]

    def rotate_heads(h, out_ref, scale):
        for hd in range(RET_HEADS):
            xh = h[:, hd * RET_DK:(hd + 1) * RET_DK]
            r = xh * cos + pltpu.roll(xh, RET_DK // 2, axis=1) * sin
            if scale != 1.0:
                r = r * scale
            out_ref[:, hd * RET_DK:(hd + 1) * RET_DK] = r

    rotate_heads(proj(_OFF_QR, RET_Q_W), qr_ref, 1.0)
    rotate_heads(proj(_OFF_KR, RET_Q_W), kr_ref, RET_DK ** -0.5)
    vr_ref[...] = proj(_OFF_VR, RET_V_W)
    g_ref[...] = proj(_OFF_G, RET_V_W)
    qa_ref[...] = proj(_OFF_QA, ATT_Q_W)
    kv = proj(_OFF_KA, 2 * ATT_KV_W)
    ka_ref[...] = kv[:, :ATT_KV_W]
    va_ref[...] = kv[:, ATT_KV_W:]


def _in_proj(x, w_bf, b, cos_t, sin_t):
    n = x.shape[0]
    tm = _row_tile(n, PROJ_ROWS)
    row = lambda w: pl.BlockSpec((tm, w), lambda i: (i, 0))
    full = lambda a: pl.BlockSpec(a.shape, lambda i: (0,) * a.ndim)
    widths = (RET_Q_W, RET_Q_W, RET_V_W, RET_V_W, ATT_Q_W, ATT_KV_W, ATT_KV_W)
    return pl.pallas_call(
        _in_proj_kernel,
        grid=(n // tm,),
        in_specs=[row(D_MODEL), full(w_bf), full(b), row(RET_DK), row(RET_DK)],
        out_specs=[row(w) for w in widths],
        out_shape=[jax.ShapeDtypeStruct((n, w), F32) for w in widths],
        compiler_params=_params(("parallel",)),
        name="in_proj",
    )(x, w_bf, b, cos_t, sin_t)


def _rope_tables(pos):
    half = RET_DK // 2
    inv_freq = ROPE_BASE ** (-jnp.arange(half, dtype=F32) / half)
    ang = pos[:, None] * inv_freq[None, :]
    c, s = jnp.cos(ang), jnp.sin(ang)
    return jnp.concatenate([c, c], axis=-1), jnp.concatenate([-s, s], axis=-1)


def _retention_log_decay():
    return jnp.log1p(-jnp.exp2(-5.0 - jnp.arange(RET_HEADS, dtype=F32)))


def _ret_prompt_kernel(q_ref, k_ref, v_ref, dec_ref, xi_ref, zeta_ref, gc_ref, o_ref, st_ref, state):
    c = pl.program_id(1)

    @pl.when(c == 0)
    def _():
        state[...] = jnp.zeros_like(state)

    q = q_ref[...].astype(BF16)
    k = k_ref[...]
    v = v_ref[...].astype(BF16)
    s_old = state[...]
    scores = lax.dot_general(q, k.astype(BF16), (((1,), (1,)), ((), ())), preferred_element_type=F32)
    scores = scores * dec_ref[...]
    o_inner = jnp.dot(scores.astype(BF16), v, preferred_element_type=F32)
    o_cross = jnp.dot(q, s_old.astype(BF16), preferred_element_type=F32) * xi_ref[...]
    o_ref[...] = o_inner + o_cross
    kz = (k * zeta_ref[...]).astype(BF16)
    upd = lax.dot_general(kz, v, (((0,), (0,)), ((), ())), preferred_element_type=F32)
    s_new = gc_ref[...] * s_old + upd
    state[...] = s_new

    @pl.when(c == pl.num_programs(1) - 1)
    def _():
        st_ref[...] = s_new


def _ret_prompt(qr, kr, vr):
    L = qr.shape[0]
    C = _row_tile(L, RET_CHUNK)
    log_g = _retention_log_decay()
    idx = jnp.arange(C, dtype=F32)
    diff = idx[:, None] - idx[None, :]
    causal = diff >= 0
    dec = jnp.where(causal[None], jnp.exp(jnp.where(causal, diff, 0.0)[None] * log_g[:, None, None]), 0.0)
    xi = jnp.exp((idx + 1.0)[None, :, None] * log_g[:, None, None])
    zeta = jnp.exp((C - 1.0 - idx)[None, :, None] * log_g[:, None, None])
    gc = jnp.broadcast_to(jnp.exp(C * log_g)[:, None, None], (RET_HEADS, 1, RET_DV))
    return pl.pallas_call(
        _ret_prompt_kernel,
        grid=(RET_HEADS, L // C),
        in_specs=[
            pl.BlockSpec((C, RET_DK), lambda h, c: (c, h)),
            pl.BlockSpec((C, RET_DK), lambda h, c: (c, h)),
            pl.BlockSpec((C, RET_DV), lambda h, c: (c, h)),
            pl.BlockSpec((None, C, C), lambda h, c: (h, 0, 0)),
            pl.BlockSpec((None, C, 1), lambda h, c: (h, 0, 0)),
            pl.BlockSpec((None, C, 1), lambda h, c: (h, 0, 0)),
            pl.BlockSpec((None, 1, RET_DV), lambda h, c: (h, 0, 0)),
        ],
        out_specs=[
            pl.BlockSpec((C, RET_DV), lambda h, c: (c, h)),
            pl.BlockSpec((None, RET_DK, RET_DV), lambda h, c: (h, 0, 0)),
        ],
        out_shape=[
            jax.ShapeDtypeStruct((L, RET_V_W), F32),
            jax.ShapeDtypeStruct((RET_HEADS, RET_DK, RET_DV), F32),
        ],
        scratch_shapes=[pltpu.VMEM((RET_DK, RET_DV), F32)],
        compiler_params=_params(("arbitrary", "arbitrary")),
        name="retention_prompt",
    )(qr, kr, vr, dec, xi, zeta, gc)


def _ret_sample_kernel(gam_ref, qt_ref, kt_ref, v_ref, s_ref, o_ref, sn_ref):
    bb = v_ref.shape[0]
    for i in range(bb):
        for h in range(RET_HEADS):
            gam = gam_ref[h]
            qc = qt_ref[h, :, i:i + 1]
            kc = kt_ref[h, :, i:i + 1]
            vrow = v_ref[i, h:h + 1, :]
            st = s_ref[i, h]
            qk = jnp.sum(qc * kc, axis=0, keepdims=True)
            cross = jnp.sum(qc * st, axis=0, keepdims=True) * gam
            o_ref[i, h:h + 1, :] = qk * vrow + cross
            sn_ref[i, h] = gam * st + kc * vrow


def _ret_sample(qr, kr, vr, state):
    db = qr.shape[0]
    bb = _row_tile(db, SAMPLE_BATCH_BLOCK)
    nb = db // bb

    def cols(t):
        return t.reshape(nb, bb, RET_HEADS, RET_DK).transpose(0, 2, 3, 1)

    gam = jnp.exp(_retention_log_decay())
    o, s_new = pl.pallas_call(
        _ret_sample_kernel,
        grid_spec=pltpu.PrefetchScalarGridSpec(
            num_scalar_prefetch=1,
            grid=(nb,),
            in_specs=[
                pl.BlockSpec((None, RET_HEADS, RET_DK, bb), lambda b, g: (b, 0, 0, 0)),
                pl.BlockSpec((None, RET_HEADS, RET_DK, bb), lambda b, g: (b, 0, 0, 0)),
                pl.BlockSpec((bb, RET_HEADS, RET_DV), lambda b, g: (b, 0, 0)),
                pl.BlockSpec((bb, RET_HEADS, RET_DK, RET_DV), lambda b, g: (b, 0, 0, 0)),
            ],
            out_specs=[
                pl.BlockSpec((bb, RET_HEADS, RET_DV), lambda b, g: (b, 0, 0)),
                pl.BlockSpec((bb, RET_HEADS, RET_DK, RET_DV), lambda b, g: (b, 0, 0, 0)),
            ],
        ),
        out_shape=[
            jax.ShapeDtypeStruct((db, RET_HEADS, RET_DV), F32),
            jax.ShapeDtypeStruct((db, RET_HEADS, RET_DK, RET_DV), F32),
        ],
        compiler_params=_params(("parallel",)),
        name="retention_sample",
    )(gam, cols(qr), cols(kr), vr.reshape(db, RET_HEADS, RET_DV), state)
    return o.reshape(db, RET_V_W), s_new


def _softmax_sink_pv(s, sink, v_bf):
    m = jnp.maximum(jnp.max(s, axis=-1, keepdims=True), sink)
    p = jnp.exp(s - m)
    denom = jnp.sum(p, axis=-1, keepdims=True) + jnp.exp(sink - m)
    pv = jnp.dot(p.astype(BF16), v_bf, preferred_element_type=F32)
    return pv / denom


def _swa_prompt_kernel(sink_ref, q_ref, kp_ref, kc_ref, vp_ref, vc_ref, o_ref):
    b = pl.program_id(0)
    blk = q_ref.shape[0]
    qi = lax.broadcasted_iota(jnp.int32, (blk, 2 * blk), 0)
    kj = lax.broadcasted_iota(jnp.int32, (blk, 2 * blk), 1)
    valid = (kj > qi + (blk - WINDOW)) & (kj <= qi + blk) & ((kj >= blk) | (b > 0))
    k2 = jnp.concatenate([kp_ref[...], kc_ref[...]], axis=0).astype(BF16)
    v2 = jnp.concatenate([vp_ref[...], vc_ref[...]], axis=0).astype(BF16)
    for hh in range(ATT_HEADS):
        j = hh // ATT_GROUP
        qh = q_ref[:, hh * ATT_HEAD_DIM:(hh + 1) * ATT_HEAD_DIM].astype(BF16)
        kh = k2[:, j * ATT_HEAD_DIM:(j + 1) * ATT_HEAD_DIM]
        vh = v2[:, j * ATT_HEAD_DIM:(j + 1) * ATT_HEAD_DIM]
        s = lax.dot_general(qh, kh, (((1,), (1,)), ((), ())), preferred_element_type=F32)
        s = jnp.where(valid, s * (ATT_HEAD_DIM ** -0.5), -jnp.inf)
        o_ref[:, hh * ATT_HEAD_DIM:(hh + 1) * ATT_HEAD_DIM] = _softmax_sink_pv(s, sink_ref[hh], vh)


def _swa_prompt(sinks, qa, ka, va):
    L = qa.shape[0]
    blk = SWA_BLOCK
    assert L % blk == 0 and blk >= WINDOW
    cur = lambda w: pl.BlockSpec((blk, w), lambda b, s: (b, 0))
    prev = lambda w: pl.BlockSpec((blk, w), lambda b, s: (jnp.maximum(b - 1, 0), 0))
    return pl.pallas_call(
        _swa_prompt_kernel,
        grid_spec=pltpu.PrefetchScalarGridSpec(
            num_scalar_prefetch=1,
            grid=(L // blk,),
            in_specs=[cur(ATT_Q_W), prev(ATT_KV_W), cur(ATT_KV_W), prev(ATT_KV_W), cur(ATT_KV_W)],
            out_specs=cur(ATT_Q_W),
        ),
        out_shape=jax.ShapeDtypeStruct((L, ATT_Q_W), F32),
        compiler_params=_params(("parallel",)),
        name="swa_prompt",
    )(sinks, qa, ka, ka, va, va)


def _swa_sample_kernel(sink_ref, q_ref, kn_ref, vn_ref, kb_ref, vb_ref, o_ref, ko_ref, vo_ref):
    bb, nbuf = kb_ref.shape[0], kb_ref.shape[1]
    for i in range(bb):
        ko_ref[i, 0:nbuf - 1, :] = kb_ref[i, 1:nbuf, :]
        ko_ref[i, nbuf - 1:nbuf, :] = kn_ref[i:i + 1, :]
        vo_ref[i, 0:nbuf - 1, :] = vb_ref[i, 1:nbuf, :]
        vo_ref[i, nbuf - 1:nbuf, :] = vn_ref[i:i + 1, :]
        kk = ko_ref[i].astype(BF16)
        vv = vo_ref[i].astype(BF16)
        q = q_ref[i].astype(BF16)
        for j in range(ATT_KV_HEADS):
            qj = q[j * ATT_GROUP:(j + 1) * ATT_GROUP]
            kj = kk[:, j * ATT_HEAD_DIM:(j + 1) * ATT_HEAD_DIM]
            vj = vv[:, j * ATT_HEAD_DIM:(j + 1) * ATT_HEAD_DIM]
            s = lax.dot_general(qj, kj, (((1,), (1,)), ((), ())), preferred_element_type=F32)
            s = s * (ATT_HEAD_DIM ** -0.5)
            sink = sink_ref[j * ATT_GROUP:(j + 1) * ATT_GROUP, :]
            o_ref[i, j * ATT_GROUP:(j + 1) * ATT_GROUP, :] = _softmax_sink_pv(s, sink, vj)


def _swa_sample(sinks, qa, ka, va, k_buf, v_buf):
    db, nbuf = k_buf.shape[0], k_buf.shape[1]
    assert nbuf == WINDOW, "single-token step with a full window-sized cache"
    bb = _row_tile(db, SAMPLE_BATCH_BLOCK)
    kb = k_buf.reshape(db, nbuf, ATT_KV_W)
    vb = v_buf.reshape(db, nbuf, ATT_KV_W)
    row = lambda w: pl.BlockSpec((bb, w), lambda b: (b, 0))
    cache = pl.BlockSpec((bb, nbuf, ATT_KV_W), lambda b: (b, 0, 0))
    heads = pl.BlockSpec((bb, ATT_HEADS, ATT_HEAD_DIM), lambda b: (b, 0, 0))
    o, ko, vo = pl.pallas_call(
        _swa_sample_kernel,
        grid=(db // bb,),
        in_specs=[pl.BlockSpec((ATT_HEADS, 1), lambda b: (0, 0)), heads, row(ATT_KV_W), row(ATT_KV_W), cache, cache],
        out_specs=[heads, cache, cache],
        out_shape=[
            jax.ShapeDtypeStruct((db, ATT_HEADS, ATT_HEAD_DIM), F32),
            jax.ShapeDtypeStruct((db, nbuf, ATT_KV_W), F32),
            jax.ShapeDtypeStruct((db, nbuf, ATT_KV_W), F32),
        ],
        compiler_params=_params(("parallel",)),
        name="swa_sample",
    )(sinks.reshape(ATT_HEADS, 1), qa.reshape(db, ATT_HEADS, ATT_HEAD_DIM), ka, va, kb, vb)
    shape5 = (db, nbuf, ATT_KV_HEADS, ATT_HEAD_DIM)
    return o.reshape(db, ATT_Q_W), ko.reshape(shape5), vo.reshape(shape5)


def _layer_norm(z, w, b):
    mu = jnp.mean(z, axis=-1, keepdims=True)
    zc = z - mu
    var = jnp.mean(zc * zc, axis=-1, keepdims=True)
    return zc * lax.rsqrt(var + LN_EPS) * w + b


def _mix_kernel(or_ref, g_ref, oa_ref, x_ref, wo_ref, gnw_ref, gnb_ref, lnw_ref, lnb_ref, wr_ref, br_ref,
                h_ref, h3_ref, idx_ref, gate_ref):
    acc = jnp.dot(oa_ref[...].astype(BF16), wo_ref[RET_V_W:, :], preferred_element_type=F32)
    for hd in range(RET_HEADS):
        sl = slice(hd * RET_DV, (hd + 1) * RET_DV)
        o = or_ref[:, sl]
        mu = jnp.mean(o, axis=-1, keepdims=True)
        oc = o - mu
        var = jnp.mean(oc * oc, axis=-1, keepdims=True)
        on = oc * lax.rsqrt(var + GN_EPS) * gnw_ref[:, sl] + gnb_ref[:, sl]
        g = g_ref[:, sl]
        r = g * jax.nn.sigmoid(g) * on
        acc = acc + jnp.dot(r.astype(BF16), wo_ref[sl, :], preferred_element_type=F32)
    h = _layer_norm(DEEPNORM_ALPHA * x_ref[...] + acc, lnw_ref[...], lnb_ref[...])
    h_ref[...] = h
    h3_ref[...] = h.reshape(h3_ref.shape)

    logits = jnp.dot(h, wr_ref[...], preferred_element_type=F32, precision=lax.Precision.HIGHEST) + br_ref[...]
    lane = lax.broadcasted_iota(jnp.int32, logits.shape, 1)
    vals, idxs = [], []
    for _ in range(TOP_K):
        m = jnp.max(logits, axis=-1, keepdims=True)
        sel = jnp.min(jnp.where(logits == m, lane, N_EXPERTS), axis=-1, keepdims=True)
        vals.append(m)
        idxs.append(sel)
        logits = jnp.where(lane == sel, -jnp.inf, logits)
    exps = [jnp.exp(v - vals[0]) for v in vals]
    tot = exps[0] + exps[1] + exps[2] + exps[3]
    for kk in range(TOP_K):
        idx_ref[:, kk:kk + 1] = idxs[kk]
        gate_ref[:, kk:kk + 1] = exps[kk] / tot


def _mix(o_r, g_r, o_a, x, wo_bf, gn_w, gn_b, ln_w, ln_b, w_router, b_router):
    n = x.shape[0]
    tm = _row_tile(n, PROJ_ROWS)
    row = lambda w: pl.BlockSpec((tm, w), lambda i: (i, 0))
    full = lambda a: pl.BlockSpec(a.shape, lambda i: (0,) * a.ndim)
    return pl.pallas_call(
        _mix_kernel,
        grid=(n // tm,),
        in_specs=[row(RET_V_W), row(RET_V_W), row(ATT_Q_W), row(D_MODEL), full(wo_bf), full(gn_w), full(gn_b),
                  full(ln_w), full(ln_b), full(w_router), full(b_router)],
        out_specs=[row(D_MODEL), pl.BlockSpec((tm, 1, D_MODEL), lambda i: (i, 0, 0)), row(TOP_K), row(TOP_K)],
        out_shape=[
            jax.ShapeDtypeStruct((n, D_MODEL), F32),
            jax.ShapeDtypeStruct((n, 1, D_MODEL), F32),
            jax.ShapeDtypeStruct((n, TOP_K), jnp.int32),
            jax.ShapeDtypeStruct((n, TOP_K), F32),
        ],
        compiler_params=_params(("parallel",)),
        name="mix_outproj_ln1_router",
    )(o_r, g_r, o_a, x, wo_bf, gn_w, gn_b, ln_w, ln_b, w_router, b_router)


def _routing_tables(top_idx, tm, n_prompt):
    nk = top_idx.size
    flat_e = top_idx.reshape(nk)
    onehot = (flat_e[:, None] == jnp.arange(N_EXPERTS, dtype=jnp.int32)[None, :]).astype(jnp.int32)
    csum = jnp.cumsum(onehot, axis=0)
    rank = jnp.sum(csum * onehot, axis=1) - 1
    counts = csum[-1]
    items = (counts + tm - 1) // tm
    item_end = jnp.cumsum(items)
    n_used = item_end[-1]
    item_start = item_end - items
    pos = (item_start * tm)[flat_e] + rank
    n_items = nk // tm + N_EXPERTS
    w_all = jnp.arange(n_items, dtype=jnp.int32)
    w = jnp.minimum(w_all, n_used - 1)
    item_e = jnp.minimum(jnp.sum((item_end[None, :] <= w[:, None]).astype(jnp.int32), axis=1), N_EXPERTS - 1)

    item_rows = jnp.where(w_all < n_used, jnp.clip(counts[item_e] - (w - item_start[item_e]) * tm, 0, tm), 0)

    order = jnp.argsort(flat_e, stable=True).astype(jnp.int32)
    starts = jnp.cumsum(counts) - counts
    local = (w - item_start[item_e])[:, None] * tm + jnp.arange(tm, dtype=jnp.int32)[None, :]
    valid = jnp.arange(tm, dtype=jnp.int32)[None, :] < item_rows[:, None]
    sorted_idx = jnp.clip(starts[item_e][:, None] + local, 0, nk - 1)
    src = jnp.where(valid, order[sorted_idx] // TOP_K, 0).astype(jnp.int32)
    n_blocks = n_items * (tm // MOE_SUB)
    blk_valid = valid.reshape(n_blocks, MOE_SUB)
    blk_nv = jnp.sum(blk_valid.astype(jnp.int32), axis=1)
    blk_np = jnp.sum((blk_valid & (src.reshape(n_blocks, MOE_SUB) < n_prompt)).astype(jnp.int32), axis=1)
    return (pos.astype(jnp.int32), item_e.astype(jnp.int32), n_used.astype(jnp.int32).reshape(1), n_items,
            item_rows.astype(jnp.int32), src.reshape(n_items * tm), blk_np, blk_nv)


def _dispatch_kernel(src_ref, np_ref, nv_ref, hp_ref, hs_ref, xg_ref, buf, sem, *, n_prompt):
    b = pl.program_id(0)
    base = b * MOE_SUB

    @pl.when(b == 0)
    def _():
        buf[...] = jnp.zeros_like(buf)

    n_p = np_ref[b]
    n_v = nv_ref[b]

    def copy_p(r):
        return pltpu.make_async_copy(hp_ref.at[src_ref[base + r]], buf.at[pl.ds(r, 1), :], sem)

    def copy_s(r):
        return pltpu.make_async_copy(hs_ref.at[src_ref[base + r] - n_prompt], buf.at[pl.ds(r, 1), :], sem)

    def sweep(wait):
        def go(c):
            if wait:
                c.wait()
            else:
                c.start()

        def group(g, carry):
            for u in range(DMA_UNROLL):
                go(copy_p(g * DMA_UNROLL + u))
            return carry

        def one_p(r, carry):
            go(copy_p(r))
            return carry

        def one_s(r, carry):
            go(copy_s(r))
            return carry

        n_g = n_p // DMA_UNROLL
        lax.fori_loop(0, n_g, group, 0)
        lax.fori_loop(n_g * DMA_UNROLL, n_p, one_p, 0)
        lax.fori_loop(n_p, n_v, one_s, 0)

    sweep(False)
    sweep(True)
    xg_ref[...] = buf[...].astype(BF16)


def _dispatch(src, blk_np, blk_nv, h3_p, h3_s):
    n_blocks = blk_np.shape[0]
    return pl.pallas_call(
        functools.partial(_dispatch_kernel, n_prompt=h3_p.shape[0]),
        grid_spec=pltpu.PrefetchScalarGridSpec(
            num_scalar_prefetch=3,
            grid=(n_blocks,),
            in_specs=[pl.BlockSpec(memory_space=pl.ANY)] * 2,
            out_specs=pl.BlockSpec((MOE_SUB, D_MODEL), lambda b, *_: (b, 0)),
            scratch_shapes=[pltpu.VMEM((MOE_SUB, D_MODEL), F32), pltpu.SemaphoreType.DMA(())],
        ),
        out_shape=jax.ShapeDtypeStruct((n_blocks * MOE_SUB, D_MODEL), BF16),
        compiler_params=_params(("arbitrary",)),
        name="moe_dispatch",
    )(src, blk_np, blk_nv, h3_p, h3_s)


def _expert_kernel(item_e_ref, n_used_ref, rows_ref, x_ref, w1g_ref, w1l_ref, b1g_ref, b1l_ref, w2_ref, b2_ref,
                   o_ref, act, *, nj):
    w = pl.program_id(0)
    s = pl.program_id(1)
    used = w < n_used_ref[0]
    n_sub = (rows_ref[w] + MOE_SUB - 1) // MOE_SUB

    def phase1(k):
        wg = w1g_ref[...].astype(BF16)
        wl = w1l_ref[...].astype(BF16)
        for sb in range(k):
            rows = slice(sb * MOE_SUB, (sb + 1) * MOE_SUB)
            x = x_ref[rows, :]
            hg = jnp.dot(x, wg, preferred_element_type=F32) + b1g_ref[...]
            hl = jnp.dot(x, wl, preferred_element_type=F32) + b1l_ref[...]
            glu = jnp.minimum(hg, SWIGLU_LIMIT)
            lin = jnp.clip(hl, -SWIGLU_LIMIT, SWIGLU_LIMIT)
            act[s, rows, :] = (glu * jax.nn.sigmoid(SWIGLU_ALPHA * glu) * (lin + 1.0)).astype(BF16)

    def phase2(k):
        w2b = w2_ref[...].astype(BF16)
        for sb in range(MOE_SUBS_PER_ITEM):
            rows = slice(sb * MOE_SUB, (sb + 1) * MOE_SUB)
            if sb < k:
                a = jnp.concatenate([act[jj, rows, :] for jj in range(nj)], axis=1)
                res = jnp.dot(a, w2b, preferred_element_type=F32) + b2_ref[...]
                o_ref[rows] = res.reshape((MOE_SUB,) + o_ref.shape[1:])
            else:
                o_ref[rows] = jnp.zeros((MOE_SUB,) + o_ref.shape[1:], F32)

    for k in range(1, MOE_SUBS_PER_ITEM + 1):
        pl.when(used & (s < nj) & (n_sub == k))(functools.partial(phase1, k))
        pl.when(used & (s >= nj) & (n_sub == k))(functools.partial(phase2, k))

    @pl.when(jnp.logical_not(used) & (s >= nj))
    def _():
        o_ref[...] = jnp.zeros_like(o_ref)


def _experts(item_e, n_used, item_rows, xg, w1, b1, w2, b2, n_items, tm):
    tf = MOE_FF
    nj = D_FF // tf
    nn = D_MODEL // tf

    def item(w, nu):
        return jnp.minimum(w, nu[0] - 1)

    def ff(w, s, nu):
        return jnp.where(w < nu[0], jnp.minimum(s, nj - 1), nj - 1)

    def col(w, s, nu):
        return jnp.where(w < nu[0], jnp.maximum(s - nj, 0), nn - 1)

    b1r = b1.reshape(N_EXPERTS, 1, 2 * D_FF)
    b2r = b2.reshape(N_EXPERTS, 1, D_MODEL)
    return pl.pallas_call(
        functools.partial(_expert_kernel, nj=nj),
        grid_spec=pltpu.PrefetchScalarGridSpec(
            num_scalar_prefetch=3,
            grid=(n_items, nj + nn),
            in_specs=[
                pl.BlockSpec((tm, D_MODEL), lambda w, s, ie, nu, nr: (item(w, nu), 0)),
                pl.BlockSpec((None, D_MODEL, tf), lambda w, s, ie, nu, nr: (ie[w], 0, ff(w, s, nu))),
                pl.BlockSpec((None, D_MODEL, tf), lambda w, s, ie, nu, nr: (ie[w], 0, nj + ff(w, s, nu))),
                pl.BlockSpec((None, 1, tf), lambda w, s, ie, nu, nr: (ie[w], 0, ff(w, s, nu))),
                pl.BlockSpec((None, 1, tf), lambda w, s, ie, nu, nr: (ie[w], 0, nj + ff(w, s, nu))),
                pl.BlockSpec((None, D_FF, tf), lambda w, s, ie, nu, nr: (ie[w], 0, col(w, s, nu))),
                pl.BlockSpec((None, 1, tf), lambda w, s, ie, nu, nr: (ie[w], 0, col(w, s, nu))),
            ],
            out_specs=pl.BlockSpec((tm, 1, tf), lambda w, s, ie, nu, nr: (w, 0, jnp.maximum(s - nj, 0))),
            scratch_shapes=[pltpu.VMEM((nj, tm, tf), BF16)],
        ),
        out_shape=jax.ShapeDtypeStruct((n_items * tm, 1, D_MODEL), F32),
        compiler_params=_params(("arbitrary", "arbitrary")),
        name="moe_experts",
    )(item_e, n_used, item_rows, xg, w1, w1, b1r, b1r, w2, b2r)


def _combine_kernel(pos_ref, gate_ref, h_ref, eo_ref, lnw_ref, lnb_ref, y_ref, buf, sem, *, tc):
    base = pl.program_id(0) * (tc * TOP_K)

    tokens_per_group = DMA_UNROLL // TOP_K

    def mk(g, u):
        t = g * tokens_per_group + u // TOP_K
        k = u % TOP_K
        return pltpu.make_async_copy(eo_ref.at[pos_ref[base + g * DMA_UNROLL + u]],
                                     buf.at[pl.ds(k * tc + t, 1), :], sem)

    def start(g, carry):
        for u in range(DMA_UNROLL):
            mk(g, u).start()
        return carry

    def wait(g, carry):
        for u in range(DMA_UNROLL):
            mk(g, u).wait()
        return carry

    lax.fori_loop(0, tc * TOP_K // DMA_UNROLL, start, 0)
    lax.fori_loop(0, tc * TOP_K // DMA_UNROLL, wait, 0)
    gates = gate_ref[...]
    f = jnp.zeros((tc, D_MODEL), F32)
    for k in range(TOP_K):
        f = f + gates[:, k:k + 1] * buf[k * tc:(k + 1) * tc, :]
    y_ref[...] = _layer_norm(DEEPNORM_ALPHA * h_ref[...] + f, lnw_ref[...], lnb_ref[...])


def _combine(pos, gates, h, eo3, ln_w, ln_b):
    n = h.shape[0]
    tc = _row_tile(n, COMBINE_ROWS)
    full = lambda a: pl.BlockSpec(a.shape, lambda i, p: (0,) * a.ndim)
    return pl.pallas_call(
        functools.partial(_combine_kernel, tc=tc),
        grid_spec=pltpu.PrefetchScalarGridSpec(
            num_scalar_prefetch=1,
            grid=(n // tc,),
            in_specs=[
                pl.BlockSpec((tc, TOP_K), lambda i, p: (i, 0)),
                pl.BlockSpec((tc, D_MODEL), lambda i, p: (i, 0)),
                pl.BlockSpec(memory_space=pl.ANY),
                full(ln_w), full(ln_b),
            ],
            out_specs=pl.BlockSpec((tc, D_MODEL), lambda i, p: (i, 0)),
            scratch_shapes=[pltpu.VMEM((tc * TOP_K, D_MODEL), F32), pltpu.SemaphoreType.DMA(())],
        ),
        out_shape=jax.ShapeDtypeStruct((n, D_MODEL), F32),
        compiler_params=_params(("arbitrary",)),
        name="moe_combine_ln2",
    )(pos, gates, h, eo3, ln_w, ln_b)


def _row2(a):
    return a.reshape(1, -1)


def kernel(x_prompt, x_sample, state_ret, cache_swa_k, cache_swa_v, w_in, b_in, sinks, gn_w, gn_b, w_out,
           ln1_w, ln1_b, w_router, b_router, w1, b1, w2, b2, ln2_w, ln2_b):
    B, L, _ = x_prompt.shape
    DB, S, _ = x_sample.shape
    assert B == 1 and S == 1 and w_in.shape[0] == DEPTH == 1
    xp = x_prompt.reshape(L, D_MODEL)
    xs = x_sample.reshape(DB, D_MODEL)
    w_in_bf = w_in[0].astype(BF16)
    w_out_bf = w_out[0].astype(BF16)
    b_in2 = _row2(b_in[0])

    cos_p, sin_p = _rope_tables(jnp.arange(L, dtype=F32))
    qr, kr, vr, g_r, qa, ka, va = _in_proj(xp, w_in_bf, b_in2, cos_p, sin_p)
    o_r, st_p = _ret_prompt(qr, kr, vr)
    o_a = _swa_prompt(sinks[0], qa, ka, va)
    rows = min(WINDOW, L)
    k_new_p = ka[L - rows:].reshape(1, 1, rows, ATT_KV_HEADS, ATT_HEAD_DIM)
    v_new_p = va[L - rows:].reshape(1, 1, rows, ATT_KV_HEADS, ATT_HEAD_DIM)

    cos_s, sin_s = _rope_tables(jnp.full((DB,), PAST_LEN, dtype=F32))
    qr_s, kr_s, vr_s, g_s, qa_s, ka_s, va_s = _in_proj(xs, w_in_bf, b_in2, cos_s, sin_s)
    o_r_s, st_s = _ret_sample(qr_s, kr_s, vr_s, state_ret[0])
    o_a_s, k_new_s, v_new_s = _swa_sample(sinks[0], qa_s, ka_s, va_s, cache_swa_k[0], cache_swa_v[0])

    mix_w = (w_out_bf, _row2(gn_w[0]), _row2(gn_b[0]), _row2(ln1_w[0]), _row2(ln1_b[0]), w_router[0],
             _row2(b_router[0]))
    h_p, h3_p, idx_p, gate_p = _mix(o_r, g_r, o_a, xp, *mix_w)
    h_s, h3_s, idx_s, gate_s = _mix(o_r_s, g_s, o_a_s, xs, *mix_w)

    top_idx = jnp.concatenate([idx_p, idx_s], axis=0)
    pos, item_e, n_used, n_items, item_rows, src, blk_np, blk_nv = _routing_tables(top_idx, MOE_ROWS, L)
    xg = _dispatch(src, blk_np, blk_nv, h3_p, h3_s)
    eo3 = _experts(item_e, n_used, item_rows, xg, w1[0], b1[0], w2[0], b2[0], n_items, MOE_ROWS)
    ln2 = (_row2(ln2_w[0]), _row2(ln2_b[0]))
    y_p = _combine(pos[:L * TOP_K], gate_p, h_p, eo3, *ln2)
    y_s = _combine(pos[L * TOP_K:], gate_s, h_s, eo3, *ln2)

    return (y_p.reshape(1, L, D_MODEL), y_s.reshape(DB, 1, D_MODEL),
            st_p.reshape(1, 1, RET_HEADS, RET_DK, RET_DV), k_new_p, v_new_p,
            st_s[None], k_new_s[None], v_new_s[None])
```

```python
import functools

import jax
import jax.numpy as jnp
import numpy as np
from jax import lax
from jax.experimental import pallas as pl
from jax.experimental.pallas import tpu as pltpu

D_MODEL = 2048
RET_HEADS = 4
RET_DK = 128
RET_DV = 256
ROPE_BASE = 10000.0
ATT_HEADS = 16
ATT_KV_HEADS = 2
ATT_GROUP = ATT_HEADS // ATT_KV_HEADS
ATT_HEAD_DIM = 64
WINDOW = 128
RET_Q_W = RET_HEADS * RET_DK
RET_V_W = RET_HEADS * RET_DV
ATT_Q_W = ATT_HEADS * ATT_HEAD_DIM
ATT_KV_W = ATT_KV_HEADS * ATT_HEAD_DIM
D_IN = 2 * RET_Q_W + 2 * RET_V_W + ATT_Q_W + 2 * ATT_KV_W
N_EXPERTS = 32
TOP_K = 4
D_FF = D_MODEL
SWIGLU_ALPHA = 1.702
SWIGLU_LIMIT = 7.0
LN_EPS = 1e-5
GN_EPS = 1e-5
DEPTH = 1
DEEPNORM_ALPHA = (2.0 * DEPTH) ** 0.25
PAST_LEN = 16384

_OFF_QR = 0
_OFF_KR = _OFF_QR + RET_Q_W
_OFF_VR = _OFF_KR + RET_Q_W
_OFF_G = _OFF_VR + RET_V_W
_OFF_QA = _OFF_G + RET_V_W
_OFF_KA = _OFF_QA + ATT_Q_W
_OFF_VA = _OFF_KA + ATT_KV_W

V7X_VMEM_BYTES = 64 * 1024 * 1024
VMEM_LIMIT = 56 * 1024 * 1024
PROJ_ROWS = 256
RET_CHUNK = 256
SWA_BLOCK = 128
MOE_SUB = 768
MOE_SUBS_PER_ITEM = 3
MOE_ROWS = MOE_SUB * MOE_SUBS_PER_ITEM
MOE_FF = 256
COMBINE_ROWS = 256
DMA_UNROLL = 8
SAMPLE_BATCH_BLOCK = 8

BF16 = jnp.bfloat16
F32 = jnp.float32


def _params(sem, vmem=VMEM_LIMIT):
    return pltpu.CompilerParams(dimension_semantics=sem, vmem_limit_bytes=vmem)


def _row_tile(n, pref):
    t = min(pref, n)
    assert n % t == 0 and (t % 8 == 0 or t == n), (n, t)
    return t


def _in_proj_kernel(x_ref, w_ref, b_ref, cos_ref, sin_ref,
                    qr_ref, kr_ref, vr_ref, g_ref, qa_ref, ka_ref, va_ref):
    x = x_ref[...].astype(BF16)

    def proj(lo, width):
        return jnp.dot(x, w_ref[:, lo:lo + width], preferred_element_type=F32) + b_ref[:, lo:lo + width]

    cos = cos_ref[...]
    sin = sin_ref[...]

    def rotate_heads(h, out_ref, scale):
        for hd in range(RET_HEADS):
            xh = h[:, hd * RET_DK:(hd + 1) * RET_DK]
            r = xh * cos + pltpu.roll(xh, RET_DK // 2, axis=1) * sin
            if scale != 1.0:
                r = r * scale
            out_ref[:, hd * RET_DK:(hd + 1) * RET_DK] = r

    rotate_heads(proj(_OFF_QR, RET_Q_W), qr_ref, 1.0)
    rotate_heads(proj(_OFF_KR, RET_Q_W), kr_ref, RET_DK ** -0.5)
    vr_ref[...] = proj(_OFF_VR, RET_V_W)
    g_ref[...] = proj(_OFF_G, RET_V_W)
    qa_ref[...] = proj(_OFF_QA, ATT_Q_W)
    kv = proj(_OFF_KA, 2 * ATT_KV_W)
    ka_ref[...] = kv[:, :ATT_KV_W]
    va_ref[...] = kv[:, ATT_KV_W:]


def _in_proj(x, w_bf, b, cos_t, sin_t):
    n = x.shape[0]
    tm = _row_tile(n, PROJ_ROWS)
    row = lambda w: pl.BlockSpec((tm, w), lambda i: (i, 0))
    full = lambda a: pl.BlockSpec(a.shape, lambda i: (0,) * a.ndim)
    widths = (RET_Q_W, RET_Q_W, RET_V_W, RET_V_W, ATT_Q_W, ATT_KV_W, ATT_KV_W)
    return pl.pallas_call(
        _in_proj_kernel,
        grid=(n // tm,),
        in_specs=[row(D_MODEL), full(w_bf), full(b), row(RET_DK), row(RET_DK)],
        out_specs=[row(w) for w in widths],
        out_shape=[jax.ShapeDtypeStruct((n, w), F32) for w in widths],
        compiler_params=_params(("parallel",)),
        name="in_proj",
    )(x, w_bf, b, cos_t, sin_t)


def _rope_tables(pos):
    half = RET_DK // 2
    inv_freq = ROPE_BASE ** (-jnp.arange(half, dtype=F32) / half)
    ang = pos[:, None] * inv_freq[None, :]
    c, s = jnp.cos(ang), jnp.sin(ang)
    return jnp.concatenate([c, c], axis=-1), jnp.concatenate([-s, s], axis=-1)


def _retention_log_decay():
    return jnp.log1p(-jnp.exp2(-5.0 - jnp.arange(RET_HEADS, dtype=F32)))


def _ret_prompt_kernel(q_ref, k_ref, v_ref, dec_ref, xi_ref, zeta_ref, gc_ref, o_ref, st_ref, state):
    c = pl.program_id(0)

    @pl.when(c == 0)
    def _():
        state[...] = jnp.zeros_like(state)

    for hd in range(RET_HEADS):
        qk = slice(hd * RET_DK, (hd + 1) * RET_DK)
        vs = slice(hd * RET_DV, (hd + 1) * RET_DV)
        q = q_ref[:, qk].astype(BF16)
        k = k_ref[:, qk]
        v = v_ref[:, vs].astype(BF16)
        s_old = state[hd]
        scores = lax.dot_general(q, k.astype(BF16), (((1,), (1,)), ((), ())), preferred_element_type=F32)
        scores = scores * dec_ref[hd]
        o_inner = jnp.dot(scores.astype(BF16), v, preferred_element_type=F32)
        o_cross = jnp.dot(q, s_old.astype(BF16), preferred_element_type=F32) * xi_ref[hd]
        o_ref[:, vs] = o_inner + o_cross
        kz = (k * zeta_ref[hd]).astype(BF16)
        upd = lax.dot_general(kz, v, (((0,), (0,)), ((), ())), preferred_element_type=F32)
        state[hd] = gc_ref[hd] * s_old + upd

    @pl.when(c == pl.num_programs(0) - 1)
    def _():
        st_ref[...] = state[...]


def _ret_prompt(qr, kr, vr):
    L = qr.shape[0]
    C = _row_tile(L, RET_CHUNK)
    log_g = _retention_log_decay()
    idx = jnp.arange(C, dtype=F32)
    diff = idx[:, None] - idx[None, :]
    causal = diff >= 0
    dec = jnp.where(causal[None], jnp.exp(jnp.where(causal, diff, 0.0)[None] * log_g[:, None, None]), 0.0)
    xi = jnp.exp((idx + 1.0)[None, :, None] * log_g[:, None, None])
    zeta = jnp.exp((C - 1.0 - idx)[None, :, None] * log_g[:, None, None])
    gc = jnp.broadcast_to(jnp.exp(C * log_g)[:, None, None], (RET_HEADS, 1, RET_DV))
    return pl.pallas_call(
        _ret_prompt_kernel,
        grid=(L // C,),
        in_specs=[
            pl.BlockSpec((C, RET_Q_W), lambda c: (c, 0)),
            pl.BlockSpec((C, RET_Q_W), lambda c: (c, 0)),
            pl.BlockSpec((C, RET_V_W), lambda c: (c, 0)),
            pl.BlockSpec((RET_HEADS, C, C), lambda c: (0, 0, 0)),
            pl.BlockSpec((RET_HEADS, C, 1), lambda c: (0, 0, 0)),
            pl.BlockSpec((RET_HEADS, C, 1), lambda c: (0, 0, 0)),
            pl.BlockSpec((RET_HEADS, 1, RET_DV), lambda c: (0, 0, 0)),
        ],
        out_specs=[
            pl.BlockSpec((C, RET_V_W), lambda c: (c, 0)),
            pl.BlockSpec((RET_HEADS, RET_DK, RET_DV), lambda c: (0, 0, 0)),
        ],
        out_shape=[
            jax.ShapeDtypeStruct((L, RET_V_W), F32),
            jax.ShapeDtypeStruct((RET_HEADS, RET_DK, RET_DV), F32),
        ],
        scratch_shapes=[pltpu.VMEM((RET_HEADS, RET_DK, RET_DV), F32)],
        compiler_params=_params(("arbitrary",)),
        name="retention_prompt",
    )(qr, kr, vr, dec, xi, zeta, gc)


def _ret_sample_kernel(gam_ref, qt_ref, kt_ref, v_ref, s_ref, o_ref, sn_ref):
    bb = v_ref.shape[0]
    for i in range(bb):
        for h in range(RET_HEADS):
            gam = gam_ref[h]
            qc = qt_ref[h, :, i:i + 1]
            kc = kt_ref[h, :, i:i + 1]
            vrow = v_ref[i, h:h + 1, :]
            st = s_ref[i, h]
            qk = jnp.sum(qc * kc, axis=0, keepdims=True)
            cross = jnp.sum(qc * st, axis=0, keepdims=True) * gam
            o_ref[i, h:h + 1, :] = qk * vrow + cross
            sn_ref[i, h] = gam * st + kc * vrow


def _ret_sample(qr, kr, vr, state):
    db = qr.shape[0]
    bb = _row_tile(db, SAMPLE_BATCH_BLOCK)
    nb = db // bb

    def cols(t):
        return t.reshape(nb, bb, RET_HEADS, RET_DK).transpose(0, 2, 3, 1)

    gam = jnp.exp(_retention_log_decay())
    o, s_new = pl.pallas_call(
        _ret_sample_kernel,
        grid_spec=pltpu.PrefetchScalarGridSpec(
            num_scalar_prefetch=1,
            grid=(nb,),
            in_specs=[
                pl.BlockSpec((None, RET_HEADS, RET_DK, bb), lambda b, g: (b, 0, 0, 0)),
                pl.BlockSpec((None, RET_HEADS, RET_DK, bb), lambda b, g: (b, 0, 0, 0)),
                pl.BlockSpec((bb, RET_HEADS, RET_DV), lambda b, g: (b, 0, 0)),
                pl.BlockSpec((bb, RET_HEADS, RET_DK, RET_DV), lambda b, g: (b, 0, 0, 0)),
            ],
            out_specs=[
                pl.BlockSpec((bb, RET_HEADS, RET_DV), lambda b, g: (b, 0, 0)),
                pl.BlockSpec((bb, RET_HEADS, RET_DK, RET_DV), lambda b, g: (b, 0, 0, 0)),
            ],
        ),
        out_shape=[
            jax.ShapeDtypeStruct((db, RET_HEADS, RET_DV), F32),
            jax.ShapeDtypeStruct((db, RET_HEADS, RET_DK, RET_DV), F32),
        ],
        compiler_params=_params(("parallel",)),
        name="retention_sample",
    )(gam, cols(qr), cols(kr), vr.reshape(db, RET_HEADS, RET_DV), state)
    return o.reshape(db, RET_V_W), s_new


def _softmax_sink_pv(s, sink, v_bf):
    m = jnp.maximum(jnp.max(s, axis=-1, keepdims=True), sink)
    p = jnp.exp(s - m)
    denom = jnp.sum(p, axis=-1, keepdims=True) + jnp.exp(sink - m)
    pv = jnp.dot(p.astype(BF16), v_bf, preferred_element_type=F32)
    return pv / denom


def _swa_prompt_kernel(sink_ref, q_ref, kp_ref, kc_ref, vp_ref, vc_ref, o_ref):
    b = pl.program_id(0)
    blk = q_ref.shape[0]
    qi = lax.broadcasted_iota(jnp.int32, (blk, 2 * blk), 0)
    kj = lax.broadcasted_iota(jnp.int32, (blk, 2 * blk), 1)
    valid = (kj > qi + (blk - WINDOW)) & (kj <= qi + blk) & ((kj >= blk) | (b > 0))
    k2 = jnp.concatenate([kp_ref[...], kc_ref[...]], axis=0).astype(BF16)
    v2 = jnp.concatenate([vp_ref[...], vc_ref[...]], axis=0).astype(BF16)
    for hh in range(ATT_HEADS):
        j = hh // ATT_GROUP
        qh = q_ref[:, hh * ATT_HEAD_DIM:(hh + 1) * ATT_HEAD_DIM].astype(BF16)
        kh = k2[:, j * ATT_HEAD_DIM:(j + 1) * ATT_HEAD_DIM]
        vh = v2[:, j * ATT_HEAD_DIM:(j + 1) * ATT_HEAD_DIM]
        s = lax.dot_general(qh, kh, (((1,), (1,)), ((), ())), preferred_element_type=F32)
        s = jnp.where(valid, s * (ATT_HEAD_DIM ** -0.5), -jnp.inf)
        o_ref[:, hh * ATT_HEAD_DIM:(hh + 1) * ATT_HEAD_DIM] = _softmax_sink_pv(s, sink_ref[hh], vh)


def _swa_prompt(sinks, qa, ka, va):
    L = qa.shape[0]
    blk = SWA_BLOCK
    assert L % blk == 0 and blk >= WINDOW
    cur = lambda w: pl.BlockSpec((blk, w), lambda b, s: (b, 0))
    prev = lambda w: pl.BlockSpec((blk, w), lambda b, s: (jnp.maximum(b - 1, 0), 0))
    return pl.pallas_call(
        _swa_prompt_kernel,
        grid_spec=pltpu.PrefetchScalarGridSpec(
            num_scalar_prefetch=1,
            grid=(L // blk,),
            in_specs=[cur(ATT_Q_W), prev(ATT_KV_W), cur(ATT_KV_W), prev(ATT_KV_W), cur(ATT_KV_W)],
            out_specs=cur(ATT_Q_W),
        ),
        out_shape=jax.ShapeDtypeStruct((L, ATT_Q_W), F32),
        compiler_params=_params(("parallel",)),
        name="swa_prompt",
    )(sinks, qa, ka, ka, va, va)


def _swa_sample_kernel(sink_ref, q_ref, kn_ref, vn_ref, kb_ref, vb_ref, o_ref, ko_ref, vo_ref):
    bb, nbuf = kb_ref.shape[0], kb_ref.shape[1]
    for i in range(bb):
        ko_ref[i, 0:nbuf - 1, :] = kb_ref[i, 1:nbuf, :]
        ko_ref[i, nbuf - 1:nbuf, :] = kn_ref[i:i + 1, :]
        vo_ref[i, 0:nbuf - 1, :] = vb_ref[i, 1:nbuf, :]
        vo_ref[i, nbuf - 1:nbuf, :] = vn_ref[i:i + 1, :]
        kk = ko_ref[i].astype(BF16)
        vv = vo_ref[i].astype(BF16)
        q = q_ref[i].astype(BF16)
        for j in range(ATT_KV_HEADS):
            qj = q[j * ATT_GROUP:(j + 1) * ATT_GROUP]
            kj = kk[:, j * ATT_HEAD_DIM:(j + 1) * ATT_HEAD_DIM]
            vj = vv[:, j * ATT_HEAD_DIM:(j + 1) * ATT_HEAD_DIM]
            s = lax.dot_general(qj, kj, (((1,), (1,)), ((), ())), preferred_element_type=F32)
            s = s * (ATT_HEAD_DIM ** -0.5)
            sink = sink_ref[j * ATT_GROUP:(j + 1) * ATT_GROUP, :]
            o_ref[i, j * ATT_GROUP:(j + 1) * ATT_GROUP, :] = _softmax_sink_pv(s, sink, vj)


def _swa_sample(sinks, qa, ka, va, k_buf, v_buf):
    db, nbuf = k_buf.shape[0], k_buf.shape[1]
    assert nbuf == WINDOW, "single-token step with a full window-sized cache"
    bb = _row_tile(db, SAMPLE_BATCH_BLOCK)
    kb = k_buf.reshape(db, nbuf, ATT_KV_W)
    vb = v_buf.reshape(db, nbuf, ATT_KV_W)
    row = lambda w: pl.BlockSpec((bb, w), lambda b: (b, 0))
    cache = pl.BlockSpec((bb, nbuf, ATT_KV_W), lambda b: (b, 0, 0))
    heads = pl.BlockSpec((bb, ATT_HEADS, ATT_HEAD_DIM), lambda b: (b, 0, 0))
    o, ko, vo = pl.pallas_call(
        _swa_sample_kernel,
        grid=(db // bb,),
        in_specs=[pl.BlockSpec((ATT_HEADS, 1), lambda b: (0, 0)), heads, row(ATT_KV_W), row(ATT_KV_W), cache, cache],
        out_specs=[heads, cache, cache],
        out_shape=[
            jax.ShapeDtypeStruct((db, ATT_HEADS, ATT_HEAD_DIM), F32),
            jax.ShapeDtypeStruct((db, nbuf, ATT_KV_W), F32),
            jax.ShapeDtypeStruct((db, nbuf, ATT_KV_W), F32),
        ],
        compiler_params=_params(("parallel",)),
        name="swa_sample",
    )(sinks.reshape(ATT_HEADS, 1), qa.reshape(db, ATT_HEADS, ATT_HEAD_DIM), ka, va, kb, vb)
    shape5 = (db, nbuf, ATT_KV_HEADS, ATT_HEAD_DIM)
    return o.reshape(db, ATT_Q_W), ko.reshape(shape5), vo.reshape(shape5)


def _layer_norm(z, w, b):
    mu = jnp.mean(z, axis=-1, keepdims=True)
    zc = z - mu
    var = jnp.mean(zc * zc, axis=-1, keepdims=True)
    return zc * lax.rsqrt(var + LN_EPS) * w + b


def _mix_kernel(or_ref, g_ref, oa_ref, x_ref, wo_ref, gnw_ref, gnb_ref, lnw_ref, lnb_ref, wrh_ref, wrl_ref, br_ref,
                h_ref, h3_ref, idx_ref, gate_ref, mixed):
    for hd in range(RET_HEADS):
        sl = slice(hd * RET_DV, (hd + 1) * RET_DV)
        o = or_ref[:, sl]
        mu = jnp.mean(o, axis=-1, keepdims=True)
        oc = o - mu
        var = jnp.mean(oc * oc, axis=-1, keepdims=True)
        on = oc * lax.rsqrt(var + GN_EPS) * gnw_ref[:, sl] + gnb_ref[:, sl]
        g = g_ref[:, sl]
        mixed[:, sl] = (g * jax.nn.sigmoid(g) * on).astype(BF16)
    mixed[:, RET_V_W:] = oa_ref[...].astype(BF16)
    acc = jnp.dot(mixed[...], wo_ref[...], preferred_element_type=F32)
    h = _layer_norm(DEEPNORM_ALPHA * x_ref[...] + acc, lnw_ref[...], lnb_ref[...])
    h_ref[...] = h
    h3_ref[...] = h.reshape(h3_ref.shape)

    h_hi = h.astype(BF16)
    h_lo = (h - h_hi.astype(F32)).astype(BF16)
    logits = (jnp.dot(h_hi, wrh_ref[...], preferred_element_type=F32)
              + (jnp.dot(h_hi, wrl_ref[...], preferred_element_type=F32)
                 + jnp.dot(h_lo, wrh_ref[...], preferred_element_type=F32))) + br_ref[...]
    lane = lax.broadcasted_iota(jnp.int32, logits.shape, 1)
    vals, idxs = [], []
    for _ in range(TOP_K):
        m = jnp.max(logits, axis=-1, keepdims=True)
        sel = jnp.min(jnp.where(logits == m, lane, N_EXPERTS), axis=-1, keepdims=True)
        vals.append(m)
        idxs.append(sel)
        logits = jnp.where(lane == sel, -jnp.inf, logits)
    exps = [jnp.exp(v - vals[0]) for v in vals]
    tot = exps[0] + exps[1] + exps[2] + exps[3]
    for kk in range(TOP_K):
        idx_ref[:, kk:kk + 1] = idxs[kk]
        gate_ref[:, kk:kk + 1] = exps[kk] / tot


def _mix(o_r, g_r, o_a, x, wo_bf, gn_w, gn_b, ln_w, ln_b, wr_hi, wr_lo, b_router):
    n = x.shape[0]
    tm = _row_tile(n, PROJ_ROWS)
    row = lambda w: pl.BlockSpec((tm, w), lambda i: (i, 0))
    full = lambda a: pl.BlockSpec(a.shape, lambda i: (0,) * a.ndim)
    return pl.pallas_call(
        _mix_kernel,
        grid=(n // tm,),
        in_specs=[row(RET_V_W), row(RET_V_W), row(ATT_Q_W), row(D_MODEL), full(wo_bf), full(gn_w), full(gn_b),
                  full(ln_w), full(ln_b), full(wr_hi), full(wr_lo), full(b_router)],
        out_specs=[row(D_MODEL), pl.BlockSpec((tm, 1, D_MODEL), lambda i: (i, 0, 0)), row(TOP_K), row(TOP_K)],
        out_shape=[
            jax.ShapeDtypeStruct((n, D_MODEL), F32),
            jax.ShapeDtypeStruct((n, 1, D_MODEL), F32),
            jax.ShapeDtypeStruct((n, TOP_K), jnp.int32),
            jax.ShapeDtypeStruct((n, TOP_K), F32),
        ],
        scratch_shapes=[pltpu.VMEM((tm, RET_V_W + ATT_Q_W), BF16)],
        compiler_params=_params(("parallel",)),
        name="mix_outproj_ln1_router",
    )(o_r, g_r, o_a, x, wo_bf, gn_w, gn_b, ln_w, ln_b, wr_hi, wr_lo, b_router)


def _routing_tables(top_idx, tm, n_prompt):
    nk = top_idx.size
    flat_e = top_idx.reshape(nk)
    onehot = (flat_e[:, None] == jnp.arange(N_EXPERTS, dtype=jnp.int32)[None, :]).astype(jnp.int32)
    csum = jnp.cumsum(onehot, axis=0)
    rank = jnp.sum(csum * onehot, axis=1) - 1
    counts = csum[-1]
    items = (counts + tm - 1) // tm
    item_end = jnp.cumsum(items)
    n_used = item_end[-1]
    item_start = item_end - items
    pos = (item_start * tm)[flat_e] + rank
    n_items = nk // tm + N_EXPERTS
    w_all = jnp.arange(n_items, dtype=jnp.int32)
    w = jnp.minimum(w_all, n_used - 1)
    item_e = jnp.minimum(jnp.sum((item_end[None, :] <= w[:, None]).astype(jnp.int32), axis=1), N_EXPERTS - 1)

    item_rows = jnp.where(w_all < n_used, jnp.clip(counts[item_e] - (w - item_start[item_e]) * tm, 0, tm), 0)

    order = jnp.argsort(flat_e, stable=True).astype(jnp.int32)
    starts = jnp.cumsum(counts) - counts
    prompt_counts = csum[n_prompt * TOP_K - 1]
    subs = tm // MOE_SUB
    sub_off = jnp.arange(subs, dtype=jnp.int32)[None, :] * MOE_SUB
    blk_local = ((w - item_start[item_e]) * tm)[:, None] + sub_off
    blk_nv = jnp.clip(item_rows[:, None] - sub_off, 0, MOE_SUB)
    blk_np = jnp.clip(prompt_counts[item_e][:, None] - blk_local, 0, blk_nv)
    blk_base = jnp.clip(starts[item_e][:, None] + blk_local, 0, nk - 1)
    flat = lambda a: a.reshape(n_items * subs).astype(jnp.int32)
    return (pos.astype(jnp.int32), item_e.astype(jnp.int32), n_used.astype(jnp.int32).reshape(1), n_items,
            item_rows.astype(jnp.int32), order, flat(blk_base), flat(blk_np), flat(blk_nv))


def _dispatch_kernel(order_ref, base_ref, np_ref, nv_ref, hp_ref, hs_ref, xg_ref, buf, sem, *, n_prompt):
    b = pl.program_id(0)
    base = base_ref[b]

    @pl.when(b == 0)
    def _():
        buf[...] = jnp.zeros_like(buf)

    n_p = np_ref[b]
    n_v = nv_ref[b]

    def token(r):
        return lax.shift_right_logical(order_ref[base + r], TOP_K.bit_length() - 1)

    def copy_p(r):
        return pltpu.make_async_copy(hp_ref.at[token(r)], buf.at[pl.ds(r, 1), :], sem)

    def copy_s(r):
        return pltpu.make_async_copy(hs_ref.at[token(r) - n_prompt], buf.at[pl.ds(r, 1), :], sem)

    def sweep(wait):
        def go(c):
            if wait:
                c.wait()
            else:
                c.start()

        def group(g, carry):
            for u in range(DMA_UNROLL):
                go(copy_p(g * DMA_UNROLL + u))
            return carry

        def one_p(r, carry):
            go(copy_p(r))
            return carry

        def one_s(r, carry):
            go(copy_s(r))
            return carry

        n_g = n_p // DMA_UNROLL
        lax.fori_loop(0, n_g, group, 0)
        lax.fori_loop(n_g * DMA_UNROLL, n_p, one_p, 0)
        lax.fori_loop(n_p, n_v, one_s, 0)

    sweep(False)
    sweep(True)
    xg_ref[...] = buf[...].astype(BF16)


def _dispatch(order, blk_base, blk_np, blk_nv, h3_p, h3_s):
    assert TOP_K & (TOP_K - 1) == 0, "pair index -> token uses a shift"
    n_blocks = blk_np.shape[0]
    return pl.pallas_call(
        functools.partial(_dispatch_kernel, n_prompt=h3_p.shape[0]),
        grid_spec=pltpu.PrefetchScalarGridSpec(
            num_scalar_prefetch=4,
            grid=(n_blocks,),
            in_specs=[pl.BlockSpec(memory_space=pl.ANY)] * 2,
            out_specs=pl.BlockSpec((MOE_SUB, D_MODEL), lambda b, *_: (b, 0)),
            scratch_shapes=[pltpu.VMEM((MOE_SUB, D_MODEL), F32), pltpu.SemaphoreType.DMA(())],
        ),
        out_shape=jax.ShapeDtypeStruct((n_blocks * MOE_SUB, D_MODEL), BF16),
        compiler_params=_params(("arbitrary",)),
        name="moe_dispatch",
    )(order, blk_base, blk_np, blk_nv, h3_p, h3_s)


def _expert_kernel(item_e_ref, n_used_ref, rows_ref, x_ref, w1g_ref, w1l_ref, b1g_ref, b1l_ref, w2_ref, b2_ref,
                   o_ref, act, *, nj):
    w = pl.program_id(0)
    s = pl.program_id(1)
    used = w < n_used_ref[0]
    n_sub = (rows_ref[w] + MOE_SUB - 1) // MOE_SUB

    def phase1(k):
        wg = w1g_ref[...].astype(BF16)
        wl = w1l_ref[...].astype(BF16)
        for sb in range(k):
            rows = slice(sb * MOE_SUB, (sb + 1) * MOE_SUB)
            x = x_ref[rows, :]
            hg = jnp.dot(x, wg, preferred_element_type=F32) + b1g_ref[...]
            hl = jnp.dot(x, wl, preferred_element_type=F32) + b1l_ref[...]
            glu = jnp.minimum(hg, SWIGLU_LIMIT)
            lin = jnp.clip(hl, -SWIGLU_LIMIT, SWIGLU_LIMIT)
            act[s, rows, :] = (glu * jax.nn.sigmoid(SWIGLU_ALPHA * glu) * (lin + 1.0)).astype(BF16)

    def phase2(k):
        w2b = w2_ref[...].astype(BF16)
        for sb in range(MOE_SUBS_PER_ITEM):
            rows = slice(sb * MOE_SUB, (sb + 1) * MOE_SUB)
            if sb < k:
                a = jnp.concatenate([act[jj, rows, :] for jj in range(nj)], axis=1)
                res = jnp.dot(a, w2b, preferred_element_type=F32) + b2_ref[...]
                o_ref[rows] = res.reshape((MOE_SUB,) + o_ref.shape[1:])
            else:
                o_ref[rows] = jnp.zeros((MOE_SUB,) + o_ref.shape[1:], F32)

    for k in range(1, MOE_SUBS_PER_ITEM + 1):
        pl.when(used & (s < nj) & (n_sub == k))(functools.partial(phase1, k))
        pl.when(used & (s >= nj) & (n_sub == k))(functools.partial(phase2, k))

    @pl.when(jnp.logical_not(used) & (s >= nj))
    def _():
        o_ref[...] = jnp.zeros_like(o_ref)


def _experts(item_e, n_used, item_rows, xg, w1, b1, w2, b2, n_items, tm):
    tf = MOE_FF
    nj = D_FF // tf
    nn = D_MODEL // tf

    def item(w, nu):
        return jnp.minimum(w, nu[0] - 1)

    def ff(w, s, nu):
        return jnp.where(w < nu[0], jnp.minimum(s, nj - 1), nj - 1)

    def col(w, s, nu):
        return jnp.where(w < nu[0], jnp.maximum(s - nj, 0), nn - 1)

    b1r = b1.reshape(N_EXPERTS, 1, 2 * D_FF)
    b2r = b2.reshape(N_EXPERTS, 1, D_MODEL)
    return pl.pallas_call(
        functools.partial(_expert_kernel, nj=nj),
        grid_spec=pltpu.PrefetchScalarGridSpec(
            num_scalar_prefetch=3,
            grid=(n_items, nj + nn),
            in_specs=[
                pl.BlockSpec((tm, D_MODEL), lambda w, s, ie, nu, nr: (item(w, nu), 0)),
                pl.BlockSpec((None, D_MODEL, tf), lambda w, s, ie, nu, nr: (ie[w], 0, ff(w, s, nu))),
                pl.BlockSpec((None, D_MODEL, tf), lambda w, s, ie, nu, nr: (ie[w], 0, nj + ff(w, s, nu))),
                pl.BlockSpec((None, 1, tf), lambda w, s, ie, nu, nr: (ie[w], 0, ff(w, s, nu))),
                pl.BlockSpec((None, 1, tf), lambda w, s, ie, nu, nr: (ie[w], 0, nj + ff(w, s, nu))),
                pl.BlockSpec((None, D_FF, tf), lambda w, s, ie, nu, nr: (ie[w], 0, col(w, s, nu))),
                pl.BlockSpec((None, 1, tf), lambda w, s, ie, nu, nr: (ie[w], 0, col(w, s, nu))),
            ],
            out_specs=pl.BlockSpec((tm, 1, tf), lambda w, s, ie, nu, nr: (w, 0, jnp.maximum(s - nj, 0))),
            scratch_shapes=[pltpu.VMEM((nj, tm, tf), BF16)],
        ),
        out_shape=jax.ShapeDtypeStruct((n_items * tm, 1, D_MODEL), F32),
        compiler_params=_params(("arbitrary", "arbitrary")),
        name="moe_experts",
    )(item_e, n_used, item_rows, xg, w1, w1, b1r, b1r, w2, b2r)


def _combine_kernel(pos_ref, gate_ref, h_ref, eo_ref, lnw_ref, lnb_ref, y_ref, buf, sem, *, tc):
    base = pl.program_id(0) * (tc * TOP_K)

    tokens_per_group = DMA_UNROLL // TOP_K

    def mk(g, u):
        t = g * tokens_per_group + u // TOP_K
        k = u % TOP_K
        return pltpu.make_async_copy(eo_ref.at[pos_ref[base + g * DMA_UNROLL + u]],
                                     buf.at[pl.ds(k * tc + t, 1), :], sem)

    def start(g, carry):
        for u in range(DMA_UNROLL):
            mk(g, u).start()
        return carry

    def wait(g, carry):
        for u in range(DMA_UNROLL):
            mk(g, u).wait()
        return carry

    lax.fori_loop(0, tc * TOP_K // DMA_UNROLL, start, 0)
    lax.fori_loop(0, tc * TOP_K // DMA_UNROLL, wait, 0)
    gates = gate_ref[...]
    f = jnp.zeros((tc, D_MODEL), F32)
    for k in range(TOP_K):
        f = f + gates[:, k:k + 1] * buf[k * tc:(k + 1) * tc, :]
    y_ref[...] = _layer_norm(DEEPNORM_ALPHA * h_ref[...] + f, lnw_ref[...], lnb_ref[...])


def _combine(pos, gates, h, eo3, ln_w, ln_b):
    n = h.shape[0]
    tc = _row_tile(n, COMBINE_ROWS)
    full = lambda a: pl.BlockSpec(a.shape, lambda i, p: (0,) * a.ndim)
    return pl.pallas_call(
        functools.partial(_combine_kernel, tc=tc),
        grid_spec=pltpu.PrefetchScalarGridSpec(
            num_scalar_prefetch=1,
            grid=(n // tc,),
            in_specs=[
                pl.BlockSpec((tc, TOP_K), lambda i, p: (i, 0)),
                pl.BlockSpec((tc, D_MODEL), lambda i, p: (i, 0)),
                pl.BlockSpec(memory_space=pl.ANY),
                full(ln_w), full(ln_b),
            ],
            out_specs=pl.BlockSpec((tc, D_MODEL), lambda i, p: (i, 0)),
            scratch_shapes=[pltpu.VMEM((tc * TOP_K, D_MODEL), F32), pltpu.SemaphoreType.DMA(())],
        ),
        out_shape=jax.ShapeDtypeStruct((n, D_MODEL), F32),
        compiler_params=_params(("arbitrary",)),
        name="moe_combine_ln2",
    )(pos, gates, h, eo3, ln_w, ln_b)


def _row2(a):
    return a.reshape(1, -1)


def kernel(x_prompt, x_sample, state_ret, cache_swa_k, cache_swa_v, w_in, b_in, sinks, gn_w, gn_b, w_out,
           ln1_w, ln1_b, w_router, b_router, w1, b1, w2, b2, ln2_w, ln2_b):
    B, L, _ = x_prompt.shape
    DB, S, _ = x_sample.shape
    assert B == 1 and S == 1 and w_in.shape[0] == DEPTH == 1
    xp = x_prompt.reshape(L, D_MODEL)
    xs = x_sample.reshape(DB, D_MODEL)
    w_in_bf = w_in[0].astype(BF16)
    w_out_bf = w_out[0].astype(BF16)
    b_in2 = _row2(b_in[0])

    cos_p, sin_p = _rope_tables(jnp.arange(L, dtype=F32))
    qr, kr, vr, g_r, qa, ka, va = _in_proj(xp, w_in_bf, b_in2, cos_p, sin_p)
    o_r, st_p = _ret_prompt(qr, kr, vr)
    o_a = _swa_prompt(sinks[0], qa, ka, va)
    rows = min(WINDOW, L)
    k_new_p = ka[L - rows:].reshape(1, 1, rows, ATT_KV_HEADS, ATT_HEAD_DIM)
    v_new_p = va[L - rows:].reshape(1, 1, rows, ATT_KV_HEADS, ATT_HEAD_DIM)

    cos_s, sin_s = _rope_tables(jnp.full((DB,), PAST_LEN, dtype=F32))
    qr_s, kr_s, vr_s, g_s, qa_s, ka_s, va_s = _in_proj(xs, w_in_bf, b_in2, cos_s, sin_s)
    o_r_s, st_s = _ret_sample(qr_s, kr_s, vr_s, state_ret[0])
    o_a_s, k_new_s, v_new_s = _swa_sample(sinks[0], qa_s, ka_s, va_s, cache_swa_k[0], cache_swa_v[0])

    wr_hi = w_router[0].astype(BF16)
    wr_lo = (w_router[0] - wr_hi.astype(F32)).astype(BF16)
    mix_w = (w_out_bf, _row2(gn_w[0]), _row2(gn_b[0]), _row2(ln1_w[0]), _row2(ln1_b[0]), wr_hi, wr_lo,
             _row2(b_router[0]))
    h_p, h3_p, idx_p, gate_p = _mix(o_r, g_r, o_a, xp, *mix_w)
    h_s, h3_s, idx_s, gate_s = _mix(o_r_s, g_s, o_a_s, xs, *mix_w)

    top_idx = jnp.concatenate([idx_p, idx_s], axis=0)
    (pos, item_e, n_used, n_items, item_rows, order, blk_base, blk_np, blk_nv) = _routing_tables(
        top_idx, MOE_ROWS, L)
    xg = _dispatch(order, blk_base, blk_np, blk_nv, h3_p, h3_s)
    eo3 = _experts(item_e, n_used, item_rows, xg, w1[0], b1[0], w2[0], b2[0], n_items, MOE_ROWS)
    ln2 = (_row2(ln2_w[0]), _row2(ln2_b[0]))
    y_p = _combine(pos[:L * TOP_K], gate_p, h_p, eo3, *ln2)
    y_s = _combine(pos[L * TOP_K:], gate_s, h_s, eo3, *ln2)

    return (y_p.reshape(1, L, D_MODEL), y_s.reshape(DB, 1, D_MODEL),
            st_p.reshape(1, 1, RET_HEADS, RET_DK, RET_DV), k_new_p, v_new_p,
            st_s[None], k_new_s[None], v_new_s[None])
```

```python
import functools

import jax
import jax.numpy as jnp
import numpy as np
from jax import lax
from jax.experimental import pallas as pl
from jax.experimental.pallas import tpu as pltpu

D_MODEL = 2048
RET_HEADS = 4
RET_DK = 128
RET_DV = 256
ROPE_BASE = 10000.0
ATT_HEADS = 16
ATT_KV_HEADS = 2
ATT_GROUP = ATT_HEADS // ATT_KV_HEADS
ATT_HEAD_DIM = 64
WINDOW = 128
RET_Q_W = RET_HEADS * RET_DK
RET_V_W = RET_HEADS * RET_DV
ATT_Q_W = ATT_HEADS * ATT_HEAD_DIM
ATT_KV_W = ATT_KV_HEADS * ATT_HEAD_DIM
D_IN = 2 * RET_Q_W + 2 * RET_V_W + ATT_Q_W + 2 * ATT_KV_W
N_EXPERTS = 32
TOP_K = 4
D_FF = D_MODEL
SWIGLU_ALPHA = 1.702
SWIGLU_LIMIT = 7.0
LN_EPS = 1e-5
GN_EPS = 1e-5
DEPTH = 1
DEEPNORM_ALPHA = (2.0 * DEPTH) ** 0.25
PAST_LEN = 16384

_OFF_QR = 0
_OFF_KR = _OFF_QR + RET_Q_W
_OFF_VR = _OFF_KR + RET_Q_W
_OFF_G = _OFF_VR + RET_V_W
_OFF_QA = _OFF_G + RET_V_W
_OFF_KA = _OFF_QA + ATT_Q_W
_OFF_VA = _OFF_KA + ATT_KV_W

V7X_VMEM_BYTES = 64 * 1024 * 1024
VMEM_LIMIT = 56 * 1024 * 1024
PROJ_ROWS = 256
RET_CHUNK = 256
SWA_BLOCK = 128
MOE_SUB = 720
MOE_SUBS_PER_ITEM = 3
MOE_ROWS = MOE_SUB * MOE_SUBS_PER_ITEM
MOE_FF = 256
COMBINE_ROWS = 256
DMA_UNROLL = 8
SAMPLE_BATCH_BLOCK = 8

BF16 = jnp.bfloat16
F32 = jnp.float32


def _params(sem, vmem=VMEM_LIMIT):
    return pltpu.CompilerParams(dimension_semantics=sem, vmem_limit_bytes=vmem)


def _row_tile(n, pref):
    t = min(pref, n)
    assert n % t == 0 and (t % 8 == 0 or t == n), (n, t)
    return t


def _in_proj_kernel(x_ref, w_ref, b_ref, cos_ref, sin_ref,
                    qr_ref, kr_ref, vr_ref, g_ref, qa_ref, ka_ref, va_ref):
    x = x_ref[...].astype(BF16)

    def proj(lo, width):
        return jnp.dot(x, w_ref[:, lo:lo + width], preferred_element_type=F32) + b_ref[:, lo:lo + width]

    cos = cos_ref[...]
    sin = sin_ref[...]

    def rotate_heads(h, out_ref, scale):
        for hd in range(RET_HEADS):
            xh = h[:, hd * RET_DK:(hd + 1) * RET_DK]
            r = xh * cos + pltpu.roll(xh, RET_DK // 2, axis=1) * sin
            if scale != 1.0:
                r = r * scale
            out_ref[:, hd * RET_DK:(hd + 1) * RET_DK] = r

    rotate_heads(proj(_OFF_QR, RET_Q_W), qr_ref, 1.0)
    rotate_heads(proj(_OFF_KR, RET_Q_W), kr_ref, RET_DK ** -0.5)
    vr_ref[...] = proj(_OFF_VR, RET_V_W)
    g_ref[...] = proj(_OFF_G, RET_V_W)
    qa_ref[...] = proj(_OFF_QA, ATT_Q_W)
    kv = proj(_OFF_KA, 2 * ATT_KV_W)
    ka_ref[...] = kv[:, :ATT_KV_W]
    va_ref[...] = kv[:, ATT_KV_W:]


def _in_proj(x, w_bf, b, cos_t, sin_t):
    n = x.shape[0]
    tm = _row_tile(n, PROJ_ROWS)
    row = lambda w: pl.BlockSpec((tm, w), lambda i: (i, 0))
    full = lambda a: pl.BlockSpec(a.shape, lambda i: (0,) * a.ndim)
    widths = (RET_Q_W, RET_Q_W, RET_V_W, RET_V_W, ATT_Q_W, ATT_KV_W, ATT_KV_W)
    return pl.pallas_call(
        _in_proj_kernel,
        grid=(n // tm,),
        in_specs=[row(D_MODEL), full(w_bf), full(b), row(RET_DK), row(RET_DK)],
        out_specs=[row(w) for w in widths],
        out_shape=[jax.ShapeDtypeStruct((n, w), F32) for w in widths],
        compiler_params=_params(("parallel",)),
        name="in_proj",
    )(x, w_bf, b, cos_t, sin_t)


def _rope_tables(pos):
    half = RET_DK // 2
    inv_freq = ROPE_BASE ** (-jnp.arange(half, dtype=F32) / half)
    ang = pos[:, None] * inv_freq[None, :]
    c, s = jnp.cos(ang), jnp.sin(ang)
    return jnp.concatenate([c, c], axis=-1), jnp.concatenate([-s, s], axis=-1)


def _retention_log_decay():
    return jnp.log1p(-jnp.exp2(-5.0 - jnp.arange(RET_HEADS, dtype=F32)))


def _ret_prompt_kernel(q_ref, k_ref, v_ref, dec_ref, xi_ref, zeta_ref, gc_ref, o_ref, st_ref, state):
    c = pl.program_id(0)

    @pl.when(c == 0)
    def _():
        state[...] = jnp.zeros_like(state)

    for hd in range(RET_HEADS):
        qk = slice(hd * RET_DK, (hd + 1) * RET_DK)
        vs = slice(hd * RET_DV, (hd + 1) * RET_DV)
        q = q_ref[:, qk].astype(BF16)
        k = k_ref[:, qk]
        v = v_ref[:, vs].astype(BF16)
        s_old = state[hd]
        scores = lax.dot_general(q, k.astype(BF16), (((1,), (1,)), ((), ())), preferred_element_type=F32)
        scores = scores * dec_ref[hd]
        o_inner = jnp.dot(scores.astype(BF16), v, preferred_element_type=F32)
        o_cross = jnp.dot(q, s_old.astype(BF16), preferred_element_type=F32) * xi_ref[hd]
        o_ref[:, vs] = o_inner + o_cross
        kz = (k * zeta_ref[hd]).astype(BF16)
        upd = lax.dot_general(kz, v, (((0,), (0,)), ((), ())), preferred_element_type=F32)
        state[hd] = gc_ref[hd] * s_old + upd

    @pl.when(c == pl.num_programs(0) - 1)
    def _():
        st_ref[...] = state[...]


def _ret_prompt(qr, kr, vr):
    L = qr.shape[0]
    C = _row_tile(L, RET_CHUNK)
    log_g = _retention_log_decay()
    idx = jnp.arange(C, dtype=F32)
    diff = idx[:, None] - idx[None, :]
    causal = diff >= 0
    dec = jnp.where(causal[None], jnp.exp(jnp.where(causal, diff, 0.0)[None] * log_g[:, None, None]), 0.0)
    xi = jnp.exp((idx + 1.0)[None, :, None] * log_g[:, None, None])
    zeta = jnp.exp((C - 1.0 - idx)[None, :, None] * log_g[:, None, None])
    gc = jnp.broadcast_to(jnp.exp(C * log_g)[:, None, None], (RET_HEADS, 1, RET_DV))
    return pl.pallas_call(
        _ret_prompt_kernel,
        grid=(L // C,),
        in_specs=[
            pl.BlockSpec((C, RET_Q_W), lambda c: (c, 0)),
            pl.BlockSpec((C, RET_Q_W), lambda c: (c, 0)),
            pl.BlockSpec((C, RET_V_W), lambda c: (c, 0)),
            pl.BlockSpec((RET_HEADS, C, C), lambda c: (0, 0, 0)),
            pl.BlockSpec((RET_HEADS, C, 1), lambda c: (0, 0, 0)),
            pl.BlockSpec((RET_HEADS, C, 1), lambda c: (0, 0, 0)),
            pl.BlockSpec((RET_HEADS, 1, RET_DV), lambda c: (0, 0, 0)),
        ],
        out_specs=[
            pl.BlockSpec((C, RET_V_W), lambda c: (c, 0)),
            pl.BlockSpec((RET_HEADS, RET_DK, RET_DV), lambda c: (0, 0, 0)),
        ],
        out_shape=[
            jax.ShapeDtypeStruct((L, RET_V_W), F32),
            jax.ShapeDtypeStruct((RET_HEADS, RET_DK, RET_DV), F32),
        ],
        scratch_shapes=[pltpu.VMEM((RET_HEADS, RET_DK, RET_DV), F32)],
        compiler_params=_params(("arbitrary",)),
        name="retention_prompt",
    )(qr, kr, vr, dec, xi, zeta, gc)


def _ret_sample_kernel(gam_ref, qt_ref, kt_ref, v_ref, s_ref, o_ref, sn_ref):
    bb = v_ref.shape[0]
    for i in range(bb):
        for h in range(RET_HEADS):
            gam = gam_ref[h]
            qc = qt_ref[h, :, i:i + 1]
            kc = kt_ref[h, :, i:i + 1]
            vrow = v_ref[i, h:h + 1, :]
            st = s_ref[i, h]
            qk = jnp.sum(qc * kc, axis=0, keepdims=True)
            cross = jnp.sum(qc * st, axis=0, keepdims=True) * gam
            o_ref[i, h:h + 1, :] = qk * vrow + cross
            sn_ref[i, h] = gam * st + kc * vrow


def _ret_sample(qr, kr, vr, state):
    db = qr.shape[0]
    bb = _row_tile(db, SAMPLE_BATCH_BLOCK)
    nb = db // bb

    def cols(t):
        return t.reshape(nb, bb, RET_HEADS, RET_DK).transpose(0, 2, 3, 1)

    gam = jnp.exp(_retention_log_decay())
    o, s_new = pl.pallas_call(
        _ret_sample_kernel,
        grid_spec=pltpu.PrefetchScalarGridSpec(
            num_scalar_prefetch=1,
            grid=(nb,),
            in_specs=[
                pl.BlockSpec((None, RET_HEADS, RET_DK, bb), lambda b, g: (b, 0, 0, 0)),
                pl.BlockSpec((None, RET_HEADS, RET_DK, bb), lambda b, g: (b, 0, 0, 0)),
                pl.BlockSpec((bb, RET_HEADS, RET_DV), lambda b, g: (b, 0, 0)),
                pl.BlockSpec((bb, RET_HEADS, RET_DK, RET_DV), lambda b, g: (b, 0, 0, 0)),
            ],
            out_specs=[
                pl.BlockSpec((bb, RET_HEADS, RET_DV), lambda b, g: (b, 0, 0)),
                pl.BlockSpec((bb, RET_HEADS, RET_DK, RET_DV), lambda b, g: (b, 0, 0, 0)),
            ],
        ),
        out_shape=[
            jax.ShapeDtypeStruct((db, RET_HEADS, RET_DV), F32),
            jax.ShapeDtypeStruct((db, RET_HEADS, RET_DK, RET_DV), F32),
        ],
        compiler_params=_params(("parallel",)),
        name="retention_sample",
    )(gam, cols(qr), cols(kr), vr.reshape(db, RET_HEADS, RET_DV), state)
    return o.reshape(db, RET_V_W), s_new


def _softmax_sink_pv(s, sink, v_bf):
    m = jnp.maximum(jnp.max(s, axis=-1, keepdims=True), sink)
    p = jnp.exp(s - m)
    denom = jnp.sum(p, axis=-1, keepdims=True) + jnp.exp(sink - m)
    pv = jnp.dot(p.astype(BF16), v_bf, preferred_element_type=F32)
    return pv / denom


def _swa_prompt_kernel(sink_ref, q_ref, kp_ref, kc_ref, vp_ref, vc_ref, o_ref):
    b = pl.program_id(0)
    blk = q_ref.shape[0]
    qi = lax.broadcasted_iota(jnp.int32, (blk, 2 * blk), 0)
    kj = lax.broadcasted_iota(jnp.int32, (blk, 2 * blk), 1)
    valid = (kj > qi + (blk - WINDOW)) & (kj <= qi + blk) & ((kj >= blk) | (b > 0))
    k2 = jnp.concatenate([kp_ref[...], kc_ref[...]], axis=0).astype(BF16)
    v2 = jnp.concatenate([vp_ref[...], vc_ref[...]], axis=0).astype(BF16)
    for hh in range(ATT_HEADS):
        j = hh // ATT_GROUP
        qh = q_ref[:, hh * ATT_HEAD_DIM:(hh + 1) * ATT_HEAD_DIM].astype(BF16)
        kh = k2[:, j * ATT_HEAD_DIM:(j + 1) * ATT_HEAD_DIM]
        vh = v2[:, j * ATT_HEAD_DIM:(j + 1) * ATT_HEAD_DIM]
        s = lax.dot_general(qh, kh, (((1,), (1,)), ((), ())), preferred_element_type=F32)
        s = jnp.where(valid, s * (ATT_HEAD_DIM ** -0.5), -jnp.inf)
        o_ref[:, hh * ATT_HEAD_DIM:(hh + 1) * ATT_HEAD_DIM] = _softmax_sink_pv(s, sink_ref[hh], vh)


def _swa_prompt(sinks, qa, ka, va):
    L = qa.shape[0]
    blk = SWA_BLOCK
    assert L % blk == 0 and blk >= WINDOW
    cur = lambda w: pl.BlockSpec((blk, w), lambda b, s: (b, 0))
    prev = lambda w: pl.BlockSpec((blk, w), lambda b, s: (jnp.maximum(b - 1, 0), 0))
    return pl.pallas_call(
        _swa_prompt_kernel,
        grid_spec=pltpu.PrefetchScalarGridSpec(
            num_scalar_prefetch=1,
            grid=(L // blk,),
            in_specs=[cur(ATT_Q_W), prev(ATT_KV_W), cur(ATT_KV_W), prev(ATT_KV_W), cur(ATT_KV_W)],
            out_specs=cur(ATT_Q_W),
        ),
        out_shape=jax.ShapeDtypeStruct((L, ATT_Q_W), F32),
        compiler_params=_params(("parallel",)),
        name="swa_prompt",
    )(sinks, qa, ka, ka, va, va)


def _swa_sample_kernel(sink_ref, q_ref, kn_ref, vn_ref, kb_ref, vb_ref, o_ref, ko_ref, vo_ref):
    bb, nbuf = kb_ref.shape[0], kb_ref.shape[1]
    for i in range(bb):
        ko_ref[i, 0:nbuf - 1, :] = kb_ref[i, 1:nbuf, :]
        ko_ref[i, nbuf - 1:nbuf, :] = kn_ref[i:i + 1, :]
        vo_ref[i, 0:nbuf - 1, :] = vb_ref[i, 1:nbuf, :]
        vo_ref[i, nbuf - 1:nbuf, :] = vn_ref[i:i + 1, :]
        kk = ko_ref[i].astype(BF16)
        vv = vo_ref[i].astype(BF16)
        q = q_ref[i].astype(BF16)
        for j in range(ATT_KV_HEADS):
            qj = q[j * ATT_GROUP:(j + 1) * ATT_GROUP]
            kj = kk[:, j * ATT_HEAD_DIM:(j + 1) * ATT_HEAD_DIM]
            vj = vv[:, j * ATT_HEAD_DIM:(j + 1) * ATT_HEAD_DIM]
            s = lax.dot_general(qj, kj, (((1,), (1,)), ((), ())), preferred_element_type=F32)
            s = s * (ATT_HEAD_DIM ** -0.5)
            sink = sink_ref[j * ATT_GROUP:(j + 1) * ATT_GROUP, :]
            o_ref[i, j * ATT_GROUP:(j + 1) * ATT_GROUP, :] = _softmax_sink_pv(s, sink, vj)


def _swa_sample(sinks, qa, ka, va, k_buf, v_buf):
    db, nbuf = k_buf.shape[0], k_buf.shape[1]
    assert nbuf == WINDOW, "single-token step with a full window-sized cache"
    bb = _row_tile(db, SAMPLE_BATCH_BLOCK)
    kb = k_buf.reshape(db, nbuf, ATT_KV_W)
    vb = v_buf.reshape(db, nbuf, ATT_KV_W)
    row = lambda w: pl.BlockSpec((bb, w), lambda b: (b, 0))
    cache = pl.BlockSpec((bb, nbuf, ATT_KV_W), lambda b: (b, 0, 0))
    heads = pl.BlockSpec((bb, ATT_HEADS, ATT_HEAD_DIM), lambda b: (b, 0, 0))
    o, ko, vo = pl.pallas_call(
        _swa_sample_kernel,
        grid=(db // bb,),
        in_specs=[pl.BlockSpec((ATT_HEADS, 1), lambda b: (0, 0)), heads, row(ATT_KV_W), row(ATT_KV_W), cache, cache],
        out_specs=[heads, cache, cache],
        out_shape=[
            jax.ShapeDtypeStruct((db, ATT_HEADS, ATT_HEAD_DIM), F32),
            jax.ShapeDtypeStruct((db, nbuf, ATT_KV_W), F32),
            jax.ShapeDtypeStruct((db, nbuf, ATT_KV_W), F32),
        ],
        compiler_params=_params(("parallel",)),
        name="swa_sample",
    )(sinks.reshape(ATT_HEADS, 1), qa.reshape(db, ATT_HEADS, ATT_HEAD_DIM), ka, va, kb, vb)
    shape5 = (db, nbuf, ATT_KV_HEADS, ATT_HEAD_DIM)
    return o.reshape(db, ATT_Q_W), ko.reshape(shape5), vo.reshape(shape5)


def _layer_norm(z, w, b):
    mu = jnp.mean(z, axis=-1, keepdims=True)
    zc = z - mu
    var = jnp.mean(zc * zc, axis=-1, keepdims=True)
    return zc * lax.rsqrt(var + LN_EPS) * w + b


def _mix_kernel(or_ref, g_ref, oa_ref, x_ref, wo_ref, gnw_ref, gnb_ref, lnw_ref, lnb_ref, wrh_ref, wrl_ref, br_ref,
                h_ref, h3_ref, idx_ref, gate_ref, mixed):
    for hd in range(RET_HEADS):
        sl = slice(hd * RET_DV, (hd + 1) * RET_DV)
        o = or_ref[:, sl]
        mu = jnp.mean(o, axis=-1, keepdims=True)
        oc = o - mu
        var = jnp.mean(oc * oc, axis=-1, keepdims=True)
        on = oc * lax.rsqrt(var + GN_EPS) * gnw_ref[:, sl] + gnb_ref[:, sl]
        g = g_ref[:, sl]
        mixed[:, sl] = (g * jax.nn.sigmoid(g) * on).astype(BF16)
    mixed[:, RET_V_W:] = oa_ref[...].astype(BF16)
    acc = jnp.dot(mixed[...], wo_ref[...], preferred_element_type=F32)
    h = _layer_norm(DEEPNORM_ALPHA * x_ref[...] + acc, lnw_ref[...], lnb_ref[...])
    h_ref[...] = h
    h3_ref[...] = h.reshape(h3_ref.shape)

    h_hi = h.astype(BF16)
    h_lo = (h - h_hi.astype(F32)).astype(BF16)
    logits = (jnp.dot(h_hi, wrh_ref[...], preferred_element_type=F32)
              + (jnp.dot(h_hi, wrl_ref[...], preferred_element_type=F32)
                 + jnp.dot(h_lo, wrh_ref[...], preferred_element_type=F32))) + br_ref[...]
    lane = lax.broadcasted_iota(jnp.int32, logits.shape, 1)
    vals, idxs = [], []
    for _ in range(TOP_K):
        m = jnp.max(logits, axis=-1, keepdims=True)
        sel = jnp.min(jnp.where(logits == m, lane, N_EXPERTS), axis=-1, keepdims=True)
        vals.append(m)
        idxs.append(sel)
        logits = jnp.where(lane == sel, -jnp.inf, logits)
    exps = [jnp.exp(v - vals[0]) for v in vals]
    tot = exps[0] + exps[1] + exps[2] + exps[3]
    for kk in range(TOP_K):
        idx_ref[:, kk:kk + 1] = idxs[kk]
        gate_ref[:, kk:kk + 1] = exps[kk] / tot


def _mix(o_r, g_r, o_a, x, wo_bf, gn_w, gn_b, ln_w, ln_b, wr_hi, wr_lo, b_router):
    n = x.shape[0]
    tm = _row_tile(n, PROJ_ROWS)
    row = lambda w: pl.BlockSpec((tm, w), lambda i: (i, 0))
    full = lambda a: pl.BlockSpec(a.shape, lambda i: (0,) * a.ndim)
    return pl.pallas_call(
        _mix_kernel,
        grid=(n // tm,),
        in_specs=[row(RET_V_W), row(RET_V_W), row(ATT_Q_W), row(D_MODEL), full(wo_bf), full(gn_w), full(gn_b),
                  full(ln_w), full(ln_b), full(wr_hi), full(wr_lo), full(b_router)],
        out_specs=[row(D_MODEL), pl.BlockSpec((tm, 1, D_MODEL), lambda i: (i, 0, 0)), row(TOP_K), row(TOP_K)],
        out_shape=[
            jax.ShapeDtypeStruct((n, D_MODEL), F32),
            jax.ShapeDtypeStruct((n, 1, D_MODEL), F32),
            jax.ShapeDtypeStruct((n, TOP_K), jnp.int32),
            jax.ShapeDtypeStruct((n, TOP_K), F32),
        ],
        scratch_shapes=[pltpu.VMEM((tm, RET_V_W + ATT_Q_W), BF16)],
        compiler_params=_params(("parallel",)),
        name="mix_outproj_ln1_router",
    )(o_r, g_r, o_a, x, wo_bf, gn_w, gn_b, ln_w, ln_b, wr_hi, wr_lo, b_router)


def _routing_tables(top_idx, tm, n_prompt):
    nk = top_idx.size
    flat_e = top_idx.reshape(nk)
    onehot = (flat_e[:, None] == jnp.arange(N_EXPERTS, dtype=jnp.int32)[None, :]).astype(jnp.int32)
    csum = jnp.cumsum(onehot, axis=0)
    rank = jnp.sum(csum * onehot, axis=1) - 1
    counts = csum[-1]
    items = (counts + tm - 1) // tm
    item_end = jnp.cumsum(items)
    n_used = item_end[-1]
    item_start = item_end - items
    pos = (item_start * tm)[flat_e] + rank
    n_items = nk // tm + N_EXPERTS
    w_all = jnp.arange(n_items, dtype=jnp.int32)
    w = jnp.minimum(w_all, n_used - 1)
    item_e = jnp.minimum(jnp.sum((item_end[None, :] <= w[:, None]).astype(jnp.int32), axis=1), N_EXPERTS - 1)

    item_rows = jnp.where(w_all < n_used, jnp.clip(counts[item_e] - (w - item_start[item_e]) * tm, 0, tm), 0)

    order = jnp.argsort(flat_e, stable=True).astype(jnp.int32)
    starts = jnp.cumsum(counts) - counts
    prompt_counts = csum[n_prompt * TOP_K - 1]
    subs = tm // MOE_SUB
    sub_off = jnp.arange(subs, dtype=jnp.int32)[None, :] * MOE_SUB
    blk_local = ((w - item_start[item_e]) * tm)[:, None] + sub_off
    blk_nv = jnp.clip(item_rows[:, None] - sub_off, 0, MOE_SUB)
    blk_np = jnp.clip(prompt_counts[item_e][:, None] - blk_local, 0, blk_nv)
    blk_base = jnp.clip(starts[item_e][:, None] + blk_local, 0, nk - 1)
    flat = lambda a: a.reshape(n_items * subs).astype(jnp.int32)
    return (pos.astype(jnp.int32), item_e.astype(jnp.int32), n_used.astype(jnp.int32).reshape(1), n_items,
            item_rows.astype(jnp.int32), order, flat(blk_base), flat(blk_np), flat(blk_nv))


def _dispatch_kernel(order_ref, base_ref, np_ref, nv_ref, hp_ref, hs_ref, xg_ref, buf, sem, *, n_prompt):
    b = pl.program_id(0)
    base = base_ref[b]

    @pl.when(b == 0)
    def _():
        buf[...] = jnp.zeros_like(buf)

    n_p = np_ref[b]
    n_v = nv_ref[b]

    def token(r):
        return lax.shift_right_logical(order_ref[base + r], TOP_K.bit_length() - 1)

    def copy_p(r):
        return pltpu.make_async_copy(hp_ref.at[token(r)], buf.at[pl.ds(r, 1), :], sem)

    def copy_s(r):
        return pltpu.make_async_copy(hs_ref.at[token(r) - n_prompt], buf.at[pl.ds(r, 1), :], sem)

    def sweep(wait):
        def go(c):
            if wait:
                c.wait()
            else:
                c.start()

        def group(g, carry):
            for u in range(DMA_UNROLL):
                go(copy_p(g * DMA_UNROLL + u))
            return carry

        def one_p(r, carry):
            go(copy_p(r))
            return carry

        def one_s(r, carry):
            go(copy_s(r))
            return carry

        n_g = n_p // DMA_UNROLL
        lax.fori_loop(0, n_g, group, 0)
        lax.fori_loop(n_g * DMA_UNROLL, n_p, one_p, 0)
        lax.fori_loop(n_p, n_v, one_s, 0)

    sweep(False)
    sweep(True)
    xg_ref[...] = buf[...].astype(BF16)


def _dispatch(order, blk_base, blk_np, blk_nv, h3_p, h3_s):
    assert TOP_K & (TOP_K - 1) == 0, "pair index -> token uses a shift"
    n_blocks = blk_np.shape[0]
    return pl.pallas_call(
        functools.partial(_dispatch_kernel, n_prompt=h3_p.shape[0]),
        grid_spec=pltpu.PrefetchScalarGridSpec(
            num_scalar_prefetch=4,
            grid=(n_blocks,),
            in_specs=[pl.BlockSpec(memory_space=pl.ANY)] * 2,
            out_specs=pl.BlockSpec((MOE_SUB, D_MODEL), lambda b, *_: (b, 0)),
            scratch_shapes=[pltpu.VMEM((MOE_SUB, D_MODEL), F32), pltpu.SemaphoreType.DMA(())],
        ),
        out_shape=jax.ShapeDtypeStruct((n_blocks * MOE_SUB, D_MODEL), BF16),
        compiler_params=_params(("arbitrary",)),
        name="moe_dispatch",
    )(order, blk_base, blk_np, blk_nv, h3_p, h3_s)


def _expert_kernel(item_e_ref, n_used_ref, rows_ref, x_ref, w1g_ref, w1l_ref, b1g_ref, b1l_ref, w2_ref, b2_ref,
                   o_ref, act, *, nj):
    w = pl.program_id(0)
    s = pl.program_id(1)
    used = w < n_used_ref[0]
    n_sub = (rows_ref[w] + MOE_SUB - 1) // MOE_SUB

    def phase1(k):
        wg = w1g_ref[...].astype(BF16)
        wl = w1l_ref[...].astype(BF16)
        for sb in range(k):
            rows = slice(sb * MOE_SUB, (sb + 1) * MOE_SUB)
            x = x_ref[rows, :]
            hg = jnp.dot(x, wg, preferred_element_type=F32) + b1g_ref[...]
            hl = jnp.dot(x, wl, preferred_element_type=F32) + b1l_ref[...]
            glu = jnp.minimum(hg, SWIGLU_LIMIT)
            lin = jnp.clip(hl, -SWIGLU_LIMIT, SWIGLU_LIMIT)
            act[s, rows, :] = (glu * jax.nn.sigmoid(SWIGLU_ALPHA * glu) * (lin + 1.0)).astype(BF16)

    def phase2(k):
        w2b = w2_ref[...].astype(BF16)
        for sb in range(MOE_SUBS_PER_ITEM):
            rows = slice(sb * MOE_SUB, (sb + 1) * MOE_SUB)
            if sb < k:
                a = jnp.concatenate([act[jj, rows, :] for jj in range(nj)], axis=1)
                res = jnp.dot(a, w2b, preferred_element_type=F32) + b2_ref[...]
                o_ref[rows, :] = res
            else:
                o_ref[rows, :] = jnp.zeros((MOE_SUB, o_ref.shape[1]), F32)

    for k in range(1, MOE_SUBS_PER_ITEM + 1):
        pl.when(used & (s < nj) & (n_sub == k))(functools.partial(phase1, k))
        pl.when(used & (s >= nj) & (n_sub == k))(functools.partial(phase2, k))

    @pl.when(jnp.logical_not(used) & (s >= nj))
    def _():
        o_ref[...] = jnp.zeros_like(o_ref)


def _experts(item_e, n_used, item_rows, xg, w1, b1, w2, b2, n_items, tm):
    tf = MOE_FF
    nj = D_FF // tf
    nn = D_MODEL // tf

    def item(w, nu):
        return jnp.minimum(w, nu[0] - 1)

    def ff(w, s, nu):
        return jnp.where(w < nu[0], jnp.minimum(s, nj - 1), nj - 1)

    def col(w, s, nu):
        return jnp.where(w < nu[0], jnp.maximum(s - nj, 0), nn - 1)

    b1r = b1.reshape(N_EXPERTS, 1, 2 * D_FF)
    b2r = b2.reshape(N_EXPERTS, 1, D_MODEL)
    return pl.pallas_call(
        functools.partial(_expert_kernel, nj=nj),
        grid_spec=pltpu.PrefetchScalarGridSpec(
            num_scalar_prefetch=3,
            grid=(n_items, nj + nn),
            in_specs=[
                pl.BlockSpec((tm, D_MODEL), lambda w, s, ie, nu, nr: (item(w, nu), 0)),
                pl.BlockSpec((None, D_MODEL, tf), lambda w, s, ie, nu, nr: (ie[w], 0, ff(w, s, nu))),
                pl.BlockSpec((None, D_MODEL, tf), lambda w, s, ie, nu, nr: (ie[w], 0, nj + ff(w, s, nu))),
                pl.BlockSpec((None, 1, tf), lambda w, s, ie, nu, nr: (ie[w], 0, ff(w, s, nu))),
                pl.BlockSpec((None, 1, tf), lambda w, s, ie, nu, nr: (ie[w], 0, nj + ff(w, s, nu))),
                pl.BlockSpec((None, D_FF, tf), lambda w, s, ie, nu, nr: (ie[w], 0, col(w, s, nu))),
                pl.BlockSpec((None, 1, tf), lambda w, s, ie, nu, nr: (ie[w], 0, col(w, s, nu))),
            ],
            out_specs=pl.BlockSpec((tm, tf), lambda w, s, ie, nu, nr: (w, jnp.maximum(s - nj, 0))),
            scratch_shapes=[pltpu.VMEM((nj, tm, tf), BF16)],
        ),
        out_shape=jax.ShapeDtypeStruct((n_items * tm, D_MODEL), F32),
        compiler_params=_params(("arbitrary", "arbitrary")),
        name="moe_experts",
    )(item_e, n_used, item_rows, xg, w1, w1, b1r, b1r, w2, b2r)


def _combine_kernel(pos_ref, gate_ref, h_ref, eo_ref, lnw_ref, lnb_ref, y_ref, buf, sem, *, tc):
    base = pl.program_id(0) * (tc * TOP_K)

    tokens_per_group = DMA_UNROLL // TOP_K

    def mk(g, u):
        t = g * tokens_per_group + u // TOP_K
        k = u % TOP_K
        return pltpu.make_async_copy(eo_ref.at[pl.ds(pos_ref[base + g * DMA_UNROLL + u], 1), :],
                                     buf.at[pl.ds(k * tc + t, 1), :], sem)

    def start(g, carry):
        for u in range(DMA_UNROLL):
            mk(g, u).start()
        return carry

    def wait(g, carry):
        for u in range(DMA_UNROLL):
            mk(g, u).wait()
        return carry

    lax.fori_loop(0, tc * TOP_K // DMA_UNROLL, start, 0)
    lax.fori_loop(0, tc * TOP_K // DMA_UNROLL, wait, 0)
    gates = gate_ref[...]
    f = jnp.zeros((tc, D_MODEL), F32)
    for k in range(TOP_K):
        f = f + gates[:, k:k + 1] * buf[k * tc:(k + 1) * tc, :]
    y_ref[...] = _layer_norm(DEEPNORM_ALPHA * h_ref[...] + f, lnw_ref[...], lnb_ref[...])


def _combine(pos, gates, h, eo3, ln_w, ln_b):
    n = h.shape[0]
    tc = _row_tile(n, COMBINE_ROWS)
    full = lambda a: pl.BlockSpec(a.shape, lambda i, p: (0,) * a.ndim)
    return pl.pallas_call(
        functools.partial(_combine_kernel, tc=tc),
        grid_spec=pltpu.PrefetchScalarGridSpec(
            num_scalar_prefetch=1,
            grid=(n // tc,),
            in_specs=[
                pl.BlockSpec((tc, TOP_K), lambda i, p: (i, 0)),
                pl.BlockSpec((tc, D_MODEL), lambda i, p: (i, 0)),
                pl.BlockSpec(memory_space=pl.ANY),
                full(ln_w), full(ln_b),
            ],
            out_specs=pl.BlockSpec((tc, D_MODEL), lambda i, p: (i, 0)),
            scratch_shapes=[pltpu.VMEM((tc * TOP_K, D_MODEL), F32), pltpu.SemaphoreType.DMA(())],
        ),
        out_shape=jax.ShapeDtypeStruct((n, D_MODEL), F32),
        compiler_params=_params(("arbitrary",)),
        name="moe_combine_ln2",
    )(pos, gates, h, eo3, ln_w, ln_b)


def _row2(a):
    return a.reshape(1, -1)


def kernel(x_prompt, x_sample, state_ret, cache_swa_k, cache_swa_v, w_in, b_in, sinks, gn_w, gn_b, w_out,
           ln1_w, ln1_b, w_router, b_router, w1, b1, w2, b2, ln2_w, ln2_b):
    B, L, _ = x_prompt.shape
    DB, S, _ = x_sample.shape
    assert B == 1 and S == 1 and w_in.shape[0] == DEPTH == 1
    xp = x_prompt.reshape(L, D_MODEL)
    xs = x_sample.reshape(DB, D_MODEL)
    w_in_bf = w_in[0].astype(BF16)
    w_out_bf = w_out[0].astype(BF16)
    b_in2 = _row2(b_in[0])

    cos_p, sin_p = _rope_tables(jnp.arange(L, dtype=F32))
    qr, kr, vr, g_r, qa, ka, va = _in_proj(xp, w_in_bf, b_in2, cos_p, sin_p)
    o_r, st_p = _ret_prompt(qr, kr, vr)
    o_a = _swa_prompt(sinks[0], qa, ka, va)
    rows = min(WINDOW, L)
    k_new_p = ka[L - rows:].reshape(1, 1, rows, ATT_KV_HEADS, ATT_HEAD_DIM)
    v_new_p = va[L - rows:].reshape(1, 1, rows, ATT_KV_HEADS, ATT_HEAD_DIM)

    cos_s, sin_s = _rope_tables(jnp.full((DB,), PAST_LEN, dtype=F32))
    qr_s, kr_s, vr_s, g_s, qa_s, ka_s, va_s = _in_proj(xs, w_in_bf, b_in2, cos_s, sin_s)
    o_r_s, st_s = _ret_sample(qr_s, kr_s, vr_s, state_ret[0])
    o_a_s, k_new_s, v_new_s = _swa_sample(sinks[0], qa_s, ka_s, va_s, cache_swa_k[0], cache_swa_v[0])

    wr_hi = w_router[0].astype(BF16)
    wr_lo = (w_router[0] - wr_hi.astype(F32)).astype(BF16)
    mix_w = (w_out_bf, _row2(gn_w[0]), _row2(gn_b[0]), _row2(ln1_w[0]), _row2(ln1_b[0]), wr_hi, wr_lo,
             _row2(b_router[0]))
    h_p, h3_p, idx_p, gate_p = _mix(o_r, g_r, o_a, xp, *mix_w)
    h_s, h3_s, idx_s, gate_s = _mix(o_r_s, g_s, o_a_s, xs, *mix_w)

    top_idx = jnp.concatenate([idx_p, idx_s], axis=0)
    (pos, item_e, n_used, n_items, item_rows, order, blk_base, blk_np, blk_nv) = _routing_tables(
        top_idx, MOE_ROWS, L)
    xg = _dispatch(order, blk_base, blk_np, blk_nv, h3_p, h3_s)
    eo3 = _experts(item_e, n_used, item_rows, xg, w1[0], b1[0], w2[0], b2[0], n_items, MOE_ROWS)
    ln2 = (_row2(ln2_w[0]), _row2(ln2_b[0]))
    y_p = _combine(pos[:L * TOP_K], gate_p, h_p, eo3, *ln2)
    y_s = _combine(pos[L * TOP_K:], gate_s, h_s, eo3, *ln2)

    return (y_p.reshape(1, L, D_MODEL), y_s.reshape(DB, 1, D_MODEL),
            st_p.reshape(1, 1, RET_HEADS, RET_DK, RET_DV), k_new_p, v_new_p,
            st_s[None], k_new_s[None], v_new_s[None])
```

```python
import functools

import jax
import jax.numpy as jnp
import numpy as np
from jax import lax
from jax.experimental import pallas as pl
from jax.experimental.pallas import tpu as pltpu

D_MODEL = 2048
RET_HEADS = 4
RET_DK = 128
RET_DV = 256
ROPE_BASE = 10000.0
ATT_HEADS = 16
ATT_KV_HEADS = 2
ATT_GROUP = ATT_HEADS // ATT_KV_HEADS
ATT_HEAD_DIM = 64
WINDOW = 128
RET_Q_W = RET_HEADS * RET_DK
RET_V_W = RET_HEADS * RET_DV
ATT_Q_W = ATT_HEADS * ATT_HEAD_DIM
ATT_KV_W = ATT_KV_HEADS * ATT_HEAD_DIM
D_IN = 2 * RET_Q_W + 2 * RET_V_W + ATT_Q_W + 2 * ATT_KV_W
N_EXPERTS = 32
TOP_K = 4
D_FF = D_MODEL
SWIGLU_ALPHA = 1.702
SWIGLU_LIMIT = 7.0
LN_EPS = 1e-5
GN_EPS = 1e-5
DEPTH = 1
DEEPNORM_ALPHA = (2.0 * DEPTH) ** 0.25
PAST_LEN = 16384

_OFF_QR = 0
_OFF_KR = _OFF_QR + RET_Q_W
_OFF_VR = _OFF_KR + RET_Q_W
_OFF_G = _OFF_VR + RET_V_W
_OFF_QA = _OFF_G + RET_V_W
_OFF_KA = _OFF_QA + ATT_Q_W
_OFF_VA = _OFF_KA + ATT_KV_W

V7X_VMEM_BYTES = 64 * 1024 * 1024
VMEM_LIMIT = 56 * 1024 * 1024
PROJ_ROWS = 256
RET_CHUNK = 256
SWA_BLOCK = 128
MOE_SUB = 720
MOE_SUBS_PER_ITEM = 3
MOE_ROWS = MOE_SUB * MOE_SUBS_PER_ITEM
MOE_SPARE_ITEMS = 4
MOE_FF = 256
COMBINE_ROWS = 256
DMA_UNROLL = 8
SAMPLE_BATCH_BLOCK = 8

BF16 = jnp.bfloat16
F32 = jnp.float32


def _params(sem, vmem=VMEM_LIMIT):
    return pltpu.CompilerParams(dimension_semantics=sem, vmem_limit_bytes=vmem)


def _row_tile(n, pref):
    t = min(pref, n)
    assert n % t == 0 and (t % 8 == 0 or t == n), (n, t)
    return t


def _in_proj_kernel(x_ref, w_ref, b_ref, cos_ref, sin_ref,
                    qr_ref, kr_ref, vr_ref, g_ref, qa_ref, ka_ref, va_ref):
    x = x_ref[...].astype(BF16)

    def proj(lo, width):
        return jnp.dot(x, w_ref[:, lo:lo + width], preferred_element_type=F32) + b_ref[:, lo:lo + width]

    cos = cos_ref[...]
    sin = sin_ref[...]

    def rotate_heads(h, out_ref, scale):
        for hd in range(RET_HEADS):
            xh = h[:, hd * RET_DK:(hd + 1) * RET_DK]
            r = xh * cos + pltpu.roll(xh, RET_DK // 2, axis=1) * sin
            if scale != 1.0:
                r = r * scale
            out_ref[:, hd * RET_DK:(hd + 1) * RET_DK] = r

    rotate_heads(proj(_OFF_QR, RET_Q_W), qr_ref, 1.0)
    rotate_heads(proj(_OFF_KR, RET_Q_W), kr_ref, RET_DK ** -0.5)
    vr_ref[...] = proj(_OFF_VR, RET_V_W)
    g_ref[...] = proj(_OFF_G, RET_V_W)
    qa_ref[...] = proj(_OFF_QA, ATT_Q_W)
    kv = proj(_OFF_KA, 2 * ATT_KV_W)
    ka_ref[...] = kv[:, :ATT_KV_W]
    va_ref[...] = kv[:, ATT_KV_W:]


def _in_proj(x, w_bf, b, cos_t, sin_t):
    n = x.shape[0]
    tm = _row_tile(n, PROJ_ROWS)
    row = lambda w: pl.BlockSpec((tm, w), lambda i: (i, 0))
    full = lambda a: pl.BlockSpec(a.shape, lambda i: (0,) * a.ndim)
    widths = (RET_Q_W, RET_Q_W, RET_V_W, RET_V_W, ATT_Q_W, ATT_KV_W, ATT_KV_W)
    return pl.pallas_call(
        _in_proj_kernel,
        grid=(n // tm,),
        in_specs=[row(D_MODEL), full(w_bf), full(b), row(RET_DK), row(RET_DK)],
        out_specs=[row(w) for w in widths],
        out_shape=[jax.ShapeDtypeStruct((n, w), F32) for w in widths],
        compiler_params=_params(("parallel",)),
        name="in_proj",
    )(x, w_bf, b, cos_t, sin_t)


def _rope_tables(pos):
    half = RET_DK // 2
    inv_freq = ROPE_BASE ** (-jnp.arange(half, dtype=F32) / half)
    ang = pos[:, None] * inv_freq[None, :]
    c, s = jnp.cos(ang), jnp.sin(ang)
    return jnp.concatenate([c, c], axis=-1), jnp.concatenate([-s, s], axis=-1)


def _retention_log_decay():
    return jnp.log1p(-jnp.exp2(-5.0 - jnp.arange(RET_HEADS, dtype=F32)))


def _ret_prompt_kernel(q_ref, k_ref, v_ref, dec_ref, xi_ref, zeta_ref, gc_ref, o_ref, st_ref, state):
    c = pl.program_id(0)

    @pl.when(c == 0)
    def _():
        state[...] = jnp.zeros_like(state)

    for hd in range(RET_HEADS):
        qk = slice(hd * RET_DK, (hd + 1) * RET_DK)
        vs = slice(hd * RET_DV, (hd + 1) * RET_DV)
        q = q_ref[:, qk].astype(BF16)
        k = k_ref[:, qk]
        v = v_ref[:, vs].astype(BF16)
        s_old = state[hd]
        scores = lax.dot_general(q, k.astype(BF16), (((1,), (1,)), ((), ())), preferred_element_type=F32)
        scores = scores * dec_ref[hd]
        o_inner = jnp.dot(scores.astype(BF16), v, preferred_element_type=F32)
        o_cross = jnp.dot(q, s_old.astype(BF16), preferred_element_type=F32) * xi_ref[hd]
        o_ref[:, vs] = o_inner + o_cross
        kz = (k * zeta_ref[hd]).astype(BF16)
        upd = lax.dot_general(kz, v, (((0,), (0,)), ((), ())), preferred_element_type=F32)
        state[hd] = gc_ref[hd] * s_old + upd

    @pl.when(c == pl.num_programs(0) - 1)
    def _():
        st_ref[...] = state[...]


def _ret_prompt(qr, kr, vr):
    L = qr.shape[0]
    C = _row_tile(L, RET_CHUNK)
    log_g = _retention_log_decay()
    idx = jnp.arange(C, dtype=F32)
    diff = idx[:, None] - idx[None, :]
    causal = diff >= 0
    dec = jnp.where(causal[None], jnp.exp(jnp.where(causal, diff, 0.0)[None] * log_g[:, None, None]), 0.0)
    xi = jnp.exp((idx + 1.0)[None, :, None] * log_g[:, None, None])
    zeta = jnp.exp((C - 1.0 - idx)[None, :, None] * log_g[:, None, None])
    gc = jnp.broadcast_to(jnp.exp(C * log_g)[:, None, None], (RET_HEADS, 1, RET_DV))
    return pl.pallas_call(
        _ret_prompt_kernel,
        grid=(L // C,),
        in_specs=[
            pl.BlockSpec((C, RET_Q_W), lambda c: (c, 0)),
            pl.BlockSpec((C, RET_Q_W), lambda c: (c, 0)),
            pl.BlockSpec((C, RET_V_W), lambda c: (c, 0)),
            pl.BlockSpec((RET_HEADS, C, C), lambda c: (0, 0, 0)),
            pl.BlockSpec((RET_HEADS, C, 1), lambda c: (0, 0, 0)),
            pl.BlockSpec((RET_HEADS, C, 1), lambda c: (0, 0, 0)),
            pl.BlockSpec((RET_HEADS, 1, RET_DV), lambda c: (0, 0, 0)),
        ],
        out_specs=[
            pl.BlockSpec((C, RET_V_W), lambda c: (c, 0)),
            pl.BlockSpec((RET_HEADS, RET_DK, RET_DV), lambda c: (0, 0, 0)),
        ],
        out_shape=[
            jax.ShapeDtypeStruct((L, RET_V_W), F32),
            jax.ShapeDtypeStruct((RET_HEADS, RET_DK, RET_DV), F32),
        ],
        scratch_shapes=[pltpu.VMEM((RET_HEADS, RET_DK, RET_DV), F32)],
        compiler_params=_params(("arbitrary",)),
        name="retention_prompt",
    )(qr, kr, vr, dec, xi, zeta, gc)


def _ret_sample_kernel(gam_ref, qt_ref, kt_ref, v_ref, s_ref, o_ref, sn_ref):
    bb = v_ref.shape[0]
    for i in range(bb):
        for h in range(RET_HEADS):
            gam = gam_ref[h]
            qc = qt_ref[h, :, i:i + 1]
            kc = kt_ref[h, :, i:i + 1]
            vrow = v_ref[i, h:h + 1, :]
            st = s_ref[i, h]
            qk = jnp.sum(qc * kc, axis=0, keepdims=True)
            cross = jnp.sum(qc * st, axis=0, keepdims=True) * gam
            o_ref[i, h:h + 1, :] = qk * vrow + cross
            sn_ref[i, h] = gam * st + kc * vrow


def _ret_sample(qr, kr, vr, state):
    db = qr.shape[0]
    bb = _row_tile(db, SAMPLE_BATCH_BLOCK)
    nb = db // bb

    def cols(t):
        return t.reshape(nb, bb, RET_HEADS, RET_DK).transpose(0, 2, 3, 1)

    gam = jnp.exp(_retention_log_decay())
    o, s_new = pl.pallas_call(
        _ret_sample_kernel,
        grid_spec=pltpu.PrefetchScalarGridSpec(
            num_scalar_prefetch=1,
            grid=(nb,),
            in_specs=[
                pl.BlockSpec((None, RET_HEADS, RET_DK, bb), lambda b, g: (b, 0, 0, 0)),
                pl.BlockSpec((None, RET_HEADS, RET_DK, bb), lambda b, g: (b, 0, 0, 0)),
                pl.BlockSpec((bb, RET_HEADS, RET_DV), lambda b, g: (b, 0, 0)),
                pl.BlockSpec((bb, RET_HEADS, RET_DK, RET_DV), lambda b, g: (b, 0, 0, 0)),
            ],
            out_specs=[
                pl.BlockSpec((bb, RET_HEADS, RET_DV), lambda b, g: (b, 0, 0)),
                pl.BlockSpec((bb, RET_HEADS, RET_DK, RET_DV), lambda b, g: (b, 0, 0, 0)),
            ],
        ),
        out_shape=[
            jax.ShapeDtypeStruct((db, RET_HEADS, RET_DV), F32),
            jax.ShapeDtypeStruct((db, RET_HEADS, RET_DK, RET_DV), F32),
        ],
        compiler_params=_params(("parallel",)),
        name="retention_sample",
    )(gam, cols(qr), cols(kr), vr.reshape(db, RET_HEADS, RET_DV), state)
    return o.reshape(db, RET_V_W), s_new


def _softmax_sink_pv(s, sink, v_bf):
    m = jnp.maximum(jnp.max(s, axis=-1, keepdims=True), sink)
    p = jnp.exp(s - m)
    denom = jnp.sum(p, axis=-1, keepdims=True) + jnp.exp(sink - m)
    pv = jnp.dot(p.astype(BF16), v_bf, preferred_element_type=F32)
    return pv / denom


def _swa_prompt_kernel(sink_ref, q_ref, kp_ref, kc_ref, vp_ref, vc_ref, o_ref):
    b = pl.program_id(0)
    blk = q_ref.shape[0]
    qi = lax.broadcasted_iota(jnp.int32, (blk, 2 * blk), 0)
    kj = lax.broadcasted_iota(jnp.int32, (blk, 2 * blk), 1)
    valid = (kj > qi + (blk - WINDOW)) & (kj <= qi + blk) & ((kj >= blk) | (b > 0))
    k2 = jnp.concatenate([kp_ref[...], kc_ref[...]], axis=0).astype(BF16)
    v2 = jnp.concatenate([vp_ref[...], vc_ref[...]], axis=0).astype(BF16)
    for hh in range(ATT_HEADS):
        j = hh // ATT_GROUP
        qh = q_ref[:, hh * ATT_HEAD_DIM:(hh + 1) * ATT_HEAD_DIM].astype(BF16)
        kh = k2[:, j * ATT_HEAD_DIM:(j + 1) * ATT_HEAD_DIM]
        vh = v2[:, j * ATT_HEAD_DIM:(j + 1) * ATT_HEAD_DIM]
        s = lax.dot_general(qh, kh, (((1,), (1,)), ((), ())), preferred_element_type=F32)
        s = jnp.where(valid, s * (ATT_HEAD_DIM ** -0.5), -jnp.inf)
        o_ref[:, hh * ATT_HEAD_DIM:(hh + 1) * ATT_HEAD_DIM] = _softmax_sink_pv(s, sink_ref[hh], vh)


def _swa_prompt(sinks, qa, ka, va):
    L = qa.shape[0]
    blk = SWA_BLOCK
    assert L % blk == 0 and blk >= WINDOW
    cur = lambda w: pl.BlockSpec((blk, w), lambda b, s: (b, 0))
    prev = lambda w: pl.BlockSpec((blk, w), lambda b, s: (jnp.maximum(b - 1, 0), 0))
    return pl.pallas_call(
        _swa_prompt_kernel,
        grid_spec=pltpu.PrefetchScalarGridSpec(
            num_scalar_prefetch=1,
            grid=(L // blk,),
            in_specs=[cur(ATT_Q_W), prev(ATT_KV_W), cur(ATT_KV_W), prev(ATT_KV_W), cur(ATT_KV_W)],
            out_specs=cur(ATT_Q_W),
        ),
        out_shape=jax.ShapeDtypeStruct((L, ATT_Q_W), F32),
        compiler_params=_params(("parallel",)),
        name="swa_prompt",
    )(sinks, qa, ka, ka, va, va)


def _swa_sample_kernel(sink_ref, q_ref, kn_ref, vn_ref, kb_ref, vb_ref, o_ref, ko_ref, vo_ref):
    bb, nbuf = kb_ref.shape[0], kb_ref.shape[1]
    for i in range(bb):
        ko_ref[i, 0:nbuf - 1, :] = kb_ref[i, 1:nbuf, :]
        ko_ref[i, nbuf - 1:nbuf, :] = kn_ref[i:i + 1, :]
        vo_ref[i, 0:nbuf - 1, :] = vb_ref[i, 1:nbuf, :]
        vo_ref[i, nbuf - 1:nbuf, :] = vn_ref[i:i + 1, :]
        kk = ko_ref[i].astype(BF16)
        vv = vo_ref[i].astype(BF16)
        q = q_ref[i].astype(BF16)
        for j in range(ATT_KV_HEADS):
            qj = q[j * ATT_GROUP:(j + 1) * ATT_GROUP]
            kj = kk[:, j * ATT_HEAD_DIM:(j + 1) * ATT_HEAD_DIM]
            vj = vv[:, j * ATT_HEAD_DIM:(j + 1) * ATT_HEAD_DIM]
            s = lax.dot_general(qj, kj, (((1,), (1,)), ((), ())), preferred_element_type=F32)
            s = s * (ATT_HEAD_DIM ** -0.5)
            sink = sink_ref[j * ATT_GROUP:(j + 1) * ATT_GROUP, :]
            o_ref[i, j * ATT_GROUP:(j + 1) * ATT_GROUP, :] = _softmax_sink_pv(s, sink, vj)


def _swa_sample(sinks, qa, ka, va, k_buf, v_buf):
    db, nbuf = k_buf.shape[0], k_buf.shape[1]
    assert nbuf == WINDOW, "single-token step with a full window-sized cache"
    bb = _row_tile(db, SAMPLE_BATCH_BLOCK)
    kb = k_buf.reshape(db, nbuf, ATT_KV_W)
    vb = v_buf.reshape(db, nbuf, ATT_KV_W)
    row = lambda w: pl.BlockSpec((bb, w), lambda b: (b, 0))
    cache = pl.BlockSpec((bb, nbuf, ATT_KV_W), lambda b: (b, 0, 0))
    heads = pl.BlockSpec((bb, ATT_HEADS, ATT_HEAD_DIM), lambda b: (b, 0, 0))
    o, ko, vo = pl.pallas_call(
        _swa_sample_kernel,
        grid=(db // bb,),
        in_specs=[pl.BlockSpec((ATT_HEADS, 1), lambda b: (0, 0)), heads, row(ATT_KV_W), row(ATT_KV_W), cache, cache],
        out_specs=[heads, cache, cache],
        out_shape=[
            jax.ShapeDtypeStruct((db, ATT_HEADS, ATT_HEAD_DIM), F32),
            jax.ShapeDtypeStruct((db, nbuf, ATT_KV_W), F32),
            jax.ShapeDtypeStruct((db, nbuf, ATT_KV_W), F32),
        ],
        compiler_params=_params(("parallel",)),
        name="swa_sample",
    )(sinks.reshape(ATT_HEADS, 1), qa.reshape(db, ATT_HEADS, ATT_HEAD_DIM), ka, va, kb, vb)
    shape5 = (db, nbuf, ATT_KV_HEADS, ATT_HEAD_DIM)
    return o.reshape(db, ATT_Q_W), ko.reshape(shape5), vo.reshape(shape5)


def _layer_norm(z, w, b):
    mu = jnp.mean(z, axis=-1, keepdims=True)
    zc = z - mu
    var = jnp.mean(zc * zc, axis=-1, keepdims=True)
    return zc * lax.rsqrt(var + LN_EPS) * w + b


def _mix_kernel(or_ref, g_ref, oa_ref, x_ref, wo_ref, gnw_ref, gnb_ref, lnw_ref, lnb_ref, wrh_ref, wrl_ref, br_ref,
                h_ref, h3_ref, idx_ref, gate_ref, mixed):
    for hd in range(RET_HEADS):
        sl = slice(hd * RET_DV, (hd + 1) * RET_DV)
        o = or_ref[:, sl]
        mu = jnp.mean(o, axis=-1, keepdims=True)
        oc = o - mu
        var = jnp.mean(oc * oc, axis=-1, keepdims=True)
        on = oc * lax.rsqrt(var + GN_EPS) * gnw_ref[:, sl] + gnb_ref[:, sl]
        g = g_ref[:, sl]
        mixed[:, sl] = (g * jax.nn.sigmoid(g) * on).astype(BF16)
    mixed[:, RET_V_W:] = oa_ref[...].astype(BF16)
    acc = jnp.dot(mixed[...], wo_ref[...], preferred_element_type=F32)
    h = _layer_norm(DEEPNORM_ALPHA * x_ref[...] + acc, lnw_ref[...], lnb_ref[...])
    h_ref[...] = h
    h3_ref[...] = h.reshape(h3_ref.shape)

    h_hi = h.astype(BF16)
    h_lo = (h - h_hi.astype(F32)).astype(BF16)
    logits = (jnp.dot(h_hi, wrh_ref[...], preferred_element_type=F32)
              + (jnp.dot(h_hi, wrl_ref[...], preferred_element_type=F32)
                 + jnp.dot(h_lo, wrh_ref[...], preferred_element_type=F32))) + br_ref[...]
    lane = lax.broadcasted_iota(jnp.int32, logits.shape, 1)
    vals, idxs = [], []
    for _ in range(TOP_K):
        m = jnp.max(logits, axis=-1, keepdims=True)
        sel = jnp.min(jnp.where(logits == m, lane, N_EXPERTS), axis=-1, keepdims=True)
        vals.append(m)
        idxs.append(sel)
        logits = jnp.where(lane == sel, -jnp.inf, logits)
    exps = [jnp.exp(v - vals[0]) for v in vals]
    tot = exps[0] + exps[1] + exps[2] + exps[3]
    for kk in range(TOP_K):
        idx_ref[:, kk:kk + 1] = idxs[kk]
        gate_ref[:, kk:kk + 1] = exps[kk] / tot


def _mix(o_r, g_r, o_a, x, wo_bf, gn_w, gn_b, ln_w, ln_b, wr_hi, wr_lo, b_router):
    n = x.shape[0]
    tm = _row_tile(n, PROJ_ROWS)
    row = lambda w: pl.BlockSpec((tm, w), lambda i: (i, 0))
    full = lambda a: pl.BlockSpec(a.shape, lambda i: (0,) * a.ndim)
    return pl.pallas_call(
        _mix_kernel,
        grid=(n // tm,),
        in_specs=[row(RET_V_W), row(RET_V_W), row(ATT_Q_W), row(D_MODEL), full(wo_bf), full(gn_w), full(gn_b),
                  full(ln_w), full(ln_b), full(wr_hi), full(wr_lo), full(b_router)],
        out_specs=[row(D_MODEL), pl.BlockSpec((tm, 1, D_MODEL), lambda i: (i, 0, 0)), row(TOP_K), row(TOP_K)],
        out_shape=[
            jax.ShapeDtypeStruct((n, D_MODEL), F32),
            jax.ShapeDtypeStruct((n, 1, D_MODEL), F32),
            jax.ShapeDtypeStruct((n, TOP_K), jnp.int32),
            jax.ShapeDtypeStruct((n, TOP_K), F32),
        ],
        scratch_shapes=[pltpu.VMEM((tm, RET_V_W + ATT_Q_W), BF16)],
        compiler_params=_params(("parallel",)),
        name="mix_outproj_ln1_router",
    )(o_r, g_r, o_a, x, wo_bf, gn_w, gn_b, ln_w, ln_b, wr_hi, wr_lo, b_router)


def _routing_common(top_idx, tm, n_prompt):
    nk = top_idx.size
    flat_e = top_idx.reshape(nk)
    onehot = (flat_e[:, None] == jnp.arange(N_EXPERTS, dtype=jnp.int32)[None, :]).astype(jnp.int32)
    csum = jnp.cumsum(onehot, axis=0)
    rank = jnp.sum(csum * onehot, axis=1) - 1
    counts = csum[-1]
    items = (counts + tm - 1) // tm
    item_end = jnp.cumsum(items)
    item_start = item_end - items
    pos = ((item_start * tm)[flat_e] + rank).astype(jnp.int32)
    order = jnp.argsort(flat_e, stable=True).astype(jnp.int32)
    starts = jnp.cumsum(counts) - counts
    prompt_counts = csum[n_prompt * TOP_K - 1]
    return dict(pos=pos, order=order, counts=counts, item_end=item_end, item_start=item_start,
                starts=starts, prompt_counts=prompt_counts, n_used=item_end[-1].astype(jnp.int32))


def _item_tables(rt, nk, tm, n_items):
    counts, item_end, item_start, starts = rt["counts"], rt["item_end"], rt["item_start"], rt["starts"]
    prompt_counts, n_used = rt["prompt_counts"], rt["n_used"]
    w_all = jnp.arange(n_items, dtype=jnp.int32)
    w = jnp.minimum(w_all, n_used - 1)
    item_e = jnp.minimum(jnp.sum((item_end[None, :] <= w[:, None]).astype(jnp.int32), axis=1), N_EXPERTS - 1)
    item_rows = jnp.where(w_all < n_used, jnp.clip(counts[item_e] - (w - item_start[item_e]) * tm, 0, tm), 0)
    subs = tm // MOE_SUB
    sub_off = jnp.arange(subs, dtype=jnp.int32)[None, :] * MOE_SUB
    blk_local = ((w - item_start[item_e]) * tm)[:, None] + sub_off
    blk_nv = jnp.clip(item_rows[:, None] - sub_off, 0, MOE_SUB)
    blk_np = jnp.clip(prompt_counts[item_e][:, None] - blk_local, 0, blk_nv)
    blk_base = jnp.clip(starts[item_e][:, None] + blk_local, 0, nk - 1)
    flat = lambda a: a.reshape(n_items * subs).astype(jnp.int32)
    return (item_e.astype(jnp.int32), item_rows.astype(jnp.int32), flat(blk_base), flat(blk_np), flat(blk_nv))


def _dispatch_kernel(order_ref, base_ref, np_ref, nv_ref, hp_ref, hs_ref, xg_ref, buf, sem, *, n_prompt):
    b = pl.program_id(0)
    base = base_ref[b]

    @pl.when(b == 0)
    def _():
        buf[...] = jnp.zeros_like(buf)

    n_p = np_ref[b]
    n_v = nv_ref[b]

    def token(r):
        return lax.shift_right_logical(order_ref[base + r], TOP_K.bit_length() - 1)

    def copy_p(r):
        return pltpu.make_async_copy(hp_ref.at[token(r)], buf.at[pl.ds(r, 1), :], sem)

    def copy_s(r):
        return pltpu.make_async_copy(hs_ref.at[token(r) - n_prompt], buf.at[pl.ds(r, 1), :], sem)

    def sweep(wait):
        def go(c):
            if wait:
                c.wait()
            else:
                c.start()

        def group(g, carry):
            for u in range(DMA_UNROLL):
                go(copy_p(g * DMA_UNROLL + u))
            return carry

        def one_p(r, carry):
            go(copy_p(r))
            return carry

        def one_s(r, carry):
            go(copy_s(r))
            return carry

        n_g = n_p // DMA_UNROLL
        lax.fori_loop(0, n_g, group, 0)
        lax.fori_loop(n_g * DMA_UNROLL, n_p, one_p, 0)
        lax.fori_loop(n_p, n_v, one_s, 0)

    sweep(False)
    sweep(True)
    xg_ref[...] = buf[...].astype(BF16)


def _dispatch(order, blk_base, blk_np, blk_nv, h3_p, h3_s):
    assert TOP_K & (TOP_K - 1) == 0, "pair index -> token uses a shift"
    n_blocks = blk_np.shape[0]
    return pl.pallas_call(
        functools.partial(_dispatch_kernel, n_prompt=h3_p.shape[0]),
        grid_spec=pltpu.PrefetchScalarGridSpec(
            num_scalar_prefetch=4,
            grid=(n_blocks,),
            in_specs=[pl.BlockSpec(memory_space=pl.ANY)] * 2,
            out_specs=pl.BlockSpec((MOE_SUB, D_MODEL), lambda b, *_: (b, 0)),
            scratch_shapes=[pltpu.VMEM((MOE_SUB, D_MODEL), F32), pltpu.SemaphoreType.DMA(())],
        ),
        out_shape=jax.ShapeDtypeStruct((n_blocks * MOE_SUB, D_MODEL), BF16),
        compiler_params=_params(("arbitrary",)),
        name="moe_dispatch",
    )(order, blk_base, blk_np, blk_nv, h3_p, h3_s)


def _expert_kernel(item_e_ref, n_used_ref, rows_ref, x_ref, w1g_ref, w1l_ref, b1g_ref, b1l_ref, w2_ref, b2_ref,
                   o_ref, act, *, nj):
    w = pl.program_id(0)
    s = pl.program_id(1)
    used = w < n_used_ref[0]
    n_sub = (rows_ref[w] + MOE_SUB - 1) // MOE_SUB

    def phase1(k):
        wg = w1g_ref[...].astype(BF16)
        wl = w1l_ref[...].astype(BF16)
        for sb in range(k):
            rows = slice(sb * MOE_SUB, (sb + 1) * MOE_SUB)
            x = x_ref[rows, :]
            hg = jnp.dot(x, wg, preferred_element_type=F32) + b1g_ref[...]
            hl = jnp.dot(x, wl, preferred_element_type=F32) + b1l_ref[...]
            glu = jnp.minimum(hg, SWIGLU_LIMIT)
            lin = jnp.clip(hl, -SWIGLU_LIMIT, SWIGLU_LIMIT)
            act[s, rows, :] = (glu * jax.nn.sigmoid(SWIGLU_ALPHA * glu) * (lin + 1.0)).astype(BF16)

    def phase2(k):
        w2b = w2_ref[...].astype(BF16)
        for sb in range(MOE_SUBS_PER_ITEM):
            rows = slice(sb * MOE_SUB, (sb + 1) * MOE_SUB)
            if sb < k:
                a = jnp.concatenate([act[jj, rows, :] for jj in range(nj)], axis=1)
                res = jnp.dot(a, w2b, preferred_element_type=F32) + b2_ref[...]
                o_ref[rows, :] = res
            else:
                o_ref[rows, :] = jnp.zeros((MOE_SUB, o_ref.shape[1]), F32)

    for k in range(1, MOE_SUBS_PER_ITEM + 1):
        pl.when(used & (s < nj) & (n_sub == k))(functools.partial(phase1, k))
        pl.when(used & (s >= nj) & (n_sub == k))(functools.partial(phase2, k))

    @pl.when(jnp.logical_not(used) & (s >= nj))
    def _():
        o_ref[...] = jnp.zeros_like(o_ref)


def _experts(item_e, n_used, item_rows, xg, w1, b1, w2, b2, n_items, tm):
    tf = MOE_FF
    nj = D_FF // tf
    nn = D_MODEL // tf

    def item(w, nu):
        return jnp.minimum(w, nu[0] - 1)

    def ff(w, s, nu):
        return jnp.where(w < nu[0], jnp.minimum(s, nj - 1), nj - 1)

    def col(w, s, nu):
        return jnp.where(w < nu[0], jnp.maximum(s - nj, 0), nn - 1)

    b1r = b1.reshape(N_EXPERTS, 1, 2 * D_FF)
    b2r = b2.reshape(N_EXPERTS, 1, D_MODEL)
    return pl.pallas_call(
        functools.partial(_expert_kernel, nj=nj),
        grid_spec=pltpu.PrefetchScalarGridSpec(
            num_scalar_prefetch=3,
            grid=(n_items, nj + nn),
            in_specs=[
                pl.BlockSpec((tm, D_MODEL), lambda w, s, ie, nu, nr: (item(w, nu), 0)),
                pl.BlockSpec((None, D_MODEL, tf), lambda w, s, ie, nu, nr: (ie[w], 0, ff(w, s, nu))),
                pl.BlockSpec((None, D_MODEL, tf), lambda w, s, ie, nu, nr: (ie[w], 0, nj + ff(w, s, nu))),
                pl.BlockSpec((None, 1, tf), lambda w, s, ie, nu, nr: (ie[w], 0, ff(w, s, nu))),
                pl.BlockSpec((None, 1, tf), lambda w, s, ie, nu, nr: (ie[w], 0, nj + ff(w, s, nu))),
                pl.BlockSpec((None, D_FF, tf), lambda w, s, ie, nu, nr: (ie[w], 0, col(w, s, nu))),
                pl.BlockSpec((None, 1, tf), lambda w, s, ie, nu, nr: (ie[w], 0, col(w, s, nu))),
            ],
            out_specs=pl.BlockSpec((tm, tf), lambda w, s, ie, nu, nr: (w, jnp.maximum(s - nj, 0))),
            scratch_shapes=[pltpu.VMEM((nj, tm, tf), BF16)],
        ),
        out_shape=jax.ShapeDtypeStruct((n_items * tm, D_MODEL), F32),
        compiler_params=_params(("arbitrary", "arbitrary")),
        name="moe_experts",
    )(item_e, n_used, item_rows, xg, w1, w1, b1r, b1r, w2, b2r)


def _combine_kernel(pos_ref, gate_ref, h_ref, eo_ref, lnw_ref, lnb_ref, y_ref, buf, sem, *, tc):
    base = pl.program_id(0) * (tc * TOP_K)

    tokens_per_group = DMA_UNROLL // TOP_K

    def mk(g, u):
        t = g * tokens_per_group + u // TOP_K
        k = u % TOP_K
        return pltpu.make_async_copy(eo_ref.at[pl.ds(pos_ref[base + g * DMA_UNROLL + u], 1), :],
                                     buf.at[pl.ds(k * tc + t, 1), :], sem)

    def start(g, carry):
        for u in range(DMA_UNROLL):
            mk(g, u).start()
        return carry

    def wait(g, carry):
        for u in range(DMA_UNROLL):
            mk(g, u).wait()
        return carry

    lax.fori_loop(0, tc * TOP_K // DMA_UNROLL, start, 0)
    lax.fori_loop(0, tc * TOP_K // DMA_UNROLL, wait, 0)
    gates = gate_ref[...]
    f = jnp.zeros((tc, D_MODEL), F32)
    for k in range(TOP_K):
        f = f + gates[:, k:k + 1] * buf[k * tc:(k + 1) * tc, :]
    y_ref[...] = _layer_norm(DEEPNORM_ALPHA * h_ref[...] + f, lnw_ref[...], lnb_ref[...])


def _combine(pos, gates, h, eo3, ln_w, ln_b):
    n = h.shape[0]
    tc = _row_tile(n, COMBINE_ROWS)
    full = lambda a: pl.BlockSpec(a.shape, lambda i, p: (0,) * a.ndim)
    return pl.pallas_call(
        functools.partial(_combine_kernel, tc=tc),
        grid_spec=pltpu.PrefetchScalarGridSpec(
            num_scalar_prefetch=1,
            grid=(n // tc,),
            in_specs=[
                pl.BlockSpec((tc, TOP_K), lambda i, p: (i, 0)),
                pl.BlockSpec((tc, D_MODEL), lambda i, p: (i, 0)),
                pl.BlockSpec(memory_space=pl.ANY),
                full(ln_w), full(ln_b),
            ],
            out_specs=pl.BlockSpec((tc, D_MODEL), lambda i, p: (i, 0)),
            scratch_shapes=[pltpu.VMEM((tc * TOP_K, D_MODEL), F32), pltpu.SemaphoreType.DMA(())],
        ),
        out_shape=jax.ShapeDtypeStruct((n, D_MODEL), F32),
        compiler_params=_params(("arbitrary",)),
        name="moe_combine_ln2",
    )(pos, gates, h, eo3, ln_w, ln_b)


def _row2(a):
    return a.reshape(1, -1)


def kernel(x_prompt, x_sample, state_ret, cache_swa_k, cache_swa_v, w_in, b_in, sinks, gn_w, gn_b, w_out,
           ln1_w, ln1_b, w_router, b_router, w1, b1, w2, b2, ln2_w, ln2_b):
    B, L, _ = x_prompt.shape
    DB, S, _ = x_sample.shape
    assert B == 1 and S == 1 and w_in.shape[0] == DEPTH == 1
    xp = x_prompt.reshape(L, D_MODEL)
    xs = x_sample.reshape(DB, D_MODEL)
    w_in_bf = w_in[0].astype(BF16)
    w_out_bf = w_out[0].astype(BF16)
    b_in2 = _row2(b_in[0])

    cos_p, sin_p = _rope_tables(jnp.arange(L, dtype=F32))
    qr, kr, vr, g_r, qa, ka, va = _in_proj(xp, w_in_bf, b_in2, cos_p, sin_p)
    o_r, st_p = _ret_prompt(qr, kr, vr)
    o_a = _swa_prompt(sinks[0], qa, ka, va)
    rows = min(WINDOW, L)
    k_new_p = ka[L - rows:].reshape(1, 1, rows, ATT_KV_HEADS, ATT_HEAD_DIM)
    v_new_p = va[L - rows:].reshape(1, 1, rows, ATT_KV_HEADS, ATT_HEAD_DIM)

    cos_s, sin_s = _rope_tables(jnp.full((DB,), PAST_LEN, dtype=F32))
    qr_s, kr_s, vr_s, g_s, qa_s, ka_s, va_s = _in_proj(xs, w_in_bf, b_in2, cos_s, sin_s)
    o_r_s, st_s = _ret_sample(qr_s, kr_s, vr_s, state_ret[0])
    o_a_s, k_new_s, v_new_s = _swa_sample(sinks[0], qa_s, ka_s, va_s, cache_swa_k[0], cache_swa_v[0])

    wr_hi = w_router[0].astype(BF16)
    wr_lo = (w_router[0] - wr_hi.astype(F32)).astype(BF16)
    mix_w = (w_out_bf, _row2(gn_w[0]), _row2(gn_b[0]), _row2(ln1_w[0]), _row2(ln1_b[0]), wr_hi, wr_lo,
             _row2(b_router[0]))
    h_p, h3_p, idx_p, gate_p = _mix(o_r, g_r, o_a, xp, *mix_w)
    h_s, h3_s, idx_s, gate_s = _mix(o_r_s, g_s, o_a_s, xs, *mix_w)

    top_idx = jnp.concatenate([idx_p, idx_s], axis=0)
    nk = top_idx.size
    rt = _routing_common(top_idx, MOE_ROWS, L)
    ln2 = (_row2(ln2_w[0]), _row2(ln2_b[0]))

    def moe(n_items, rt, h3_p, h3_s, h_p, h_s, gate_p, gate_s, w1e, b1e, w2e, b2e, ln_w, ln_b):
        item_e, item_rows, blk_base, blk_np, blk_nv = _item_tables(rt, nk, MOE_ROWS, n_items)
        xg = _dispatch(rt["order"], blk_base, blk_np, blk_nv, h3_p, h3_s)
        eo = _experts(item_e, rt["n_used"].reshape(1), item_rows, xg, w1e, b1e, w2e, b2e, n_items, MOE_ROWS)
        return (_combine(rt["pos"][:L * TOP_K], gate_p, h_p, eo, ln_w, ln_b),
                _combine(rt["pos"][L * TOP_K:], gate_s, h_s, eo, ln_w, ln_b))

    max_items = nk // MOE_ROWS + N_EXPERTS
    small_items = min(max_items, N_EXPERTS + MOE_SPARE_ITEMS)
    operands = (rt, h3_p, h3_s, h_p, h_s, gate_p, gate_s, w1[0], b1[0], w2[0], b2[0], *ln2)
    y_p, y_s = lax.cond(rt["n_used"] <= small_items,
                        functools.partial(moe, small_items), functools.partial(moe, max_items), *operands)

    return (y_p.reshape(1, L, D_MODEL), y_s.reshape(DB, 1, D_MODEL),
            st_p.reshape(1, 1, RET_HEADS, RET_DK, RET_DV), k_new_p, v_new_p,
            st_s[None], k_new_s[None], v_new_s[None])
```

```python
import functools

import jax
import jax.numpy as jnp
from jax import lax
from jax.experimental import pallas as pl
from jax.experimental.pallas import tpu as pltpu

D_MODEL = 2048
RET_HEADS = 4
RET_DK = 128
RET_DV = 256
ROPE_BASE = 10000.0
ATT_HEADS = 16
ATT_KV_HEADS = 2
ATT_GROUP = ATT_HEADS // ATT_KV_HEADS
ATT_HEAD_DIM = 64
WINDOW = 128
RET_Q_W = RET_HEADS * RET_DK
RET_V_W = RET_HEADS * RET_DV
ATT_Q_W = ATT_HEADS * ATT_HEAD_DIM
ATT_KV_W = ATT_KV_HEADS * ATT_HEAD_DIM
D_IN = 2 * RET_Q_W + 2 * RET_V_W + ATT_Q_W + 2 * ATT_KV_W
N_EXPERTS = 32
TOP_K = 4
D_FF = D_MODEL
SWIGLU_ALPHA = 1.702
SWIGLU_LIMIT = 7.0
LN_EPS = 1e-5
GN_EPS = 1e-5
DEPTH = 1
DEEPNORM_ALPHA = (2.0 * DEPTH) ** 0.25
PAST_LEN = 16384

_OFF_QR = 0
_OFF_KR = _OFF_QR + RET_Q_W
_OFF_VR = _OFF_KR + RET_Q_W
_OFF_G = _OFF_VR + RET_V_W
_OFF_QA = _OFF_G + RET_V_W
_OFF_KA = _OFF_QA + ATT_Q_W
_OFF_VA = _OFF_KA + ATT_KV_W

V7X_VMEM_BYTES = 64 * 1024 * 1024
VMEM_LIMIT = V7X_VMEM_BYTES - 8 * 1024 * 1024
PROJ_ROWS = 256
RET_CHUNK = 256
SWA_BLOCK = 128
MOE_SUB = 720
MOE_SUBS_PER_ITEM = 3
MOE_ROWS = MOE_SUB * MOE_SUBS_PER_ITEM
MOE_SPARE_ITEMS = 3
MOE_FF = 256
COMBINE_ROWS = 512
DMA_UNROLL = 8
SAMPLE_BATCH_BLOCK = 8

BF16 = jnp.bfloat16
F32 = jnp.float32


def _params(sem, vmem=VMEM_LIMIT):
    return pltpu.CompilerParams(dimension_semantics=sem, vmem_limit_bytes=vmem)


def _row_tile(n, pref):
    t = min(pref, n)
    assert n % t == 0 and (t % 8 == 0 or t == n), (n, t)
    return t


def _in_proj_kernel(x_ref, w_ref, b_ref, cos_ref, sin_ref,
                    qr_ref, kr_ref, vr_ref, g_ref, qa_ref, ka_ref, va_ref):
    x = x_ref[...].astype(BF16)

    def proj(lo, width):
        return jnp.dot(x, w_ref[:, lo:lo + width], preferred_element_type=F32) + b_ref[:, lo:lo + width]

    cos = cos_ref[...]
    sin = sin_ref[...]

    def rotate_heads(h, out_ref, scale):
        for hd in range(RET_HEADS):
            xh = h[:, hd * RET_DK:(hd + 1) * RET_DK]
            r = xh * cos + pltpu.roll(xh, RET_DK // 2, axis=1) * sin
            if scale != 1.0:
                r = r * scale
            out_ref[:, hd * RET_DK:(hd + 1) * RET_DK] = r

    rotate_heads(proj(_OFF_QR, RET_Q_W), qr_ref, 1.0)
    rotate_heads(proj(_OFF_KR, RET_Q_W), kr_ref, RET_DK ** -0.5)
    vr_ref[...] = proj(_OFF_VR, RET_V_W)
    g_ref[...] = proj(_OFF_G, RET_V_W)
    qa_ref[...] = proj(_OFF_QA, ATT_Q_W)
    kv = proj(_OFF_KA, 2 * ATT_KV_W)
    ka_ref[...] = kv[:, :ATT_KV_W]
    va_ref[...] = kv[:, ATT_KV_W:]


def _in_proj(x, w_bf, b, cos_t, sin_t):
    n = x.shape[0]
    tm = _row_tile(n, PROJ_ROWS)
    row = lambda w: pl.BlockSpec((tm, w), lambda i: (i, 0))
    full = lambda a: pl.BlockSpec(a.shape, lambda i: (0,) * a.ndim)
    widths = (RET_Q_W, RET_Q_W, RET_V_W, RET_V_W, ATT_Q_W, ATT_KV_W, ATT_KV_W)
    return pl.pallas_call(
        _in_proj_kernel,
        grid=(n // tm,),
        in_specs=[row(D_MODEL), full(w_bf), full(b), row(RET_DK), row(RET_DK)],
        out_specs=[row(w) for w in widths],
        out_shape=[jax.ShapeDtypeStruct((n, w), F32) for w in widths],
        compiler_params=_params(("parallel",)),
        name="in_proj",
    )(x, w_bf, b, cos_t, sin_t)


def _rope_tables(pos):
    half = RET_DK // 2
    inv_freq = ROPE_BASE ** (-jnp.arange(half, dtype=F32) / half)
    ang = pos[:, None] * inv_freq[None, :]
    c, s = jnp.cos(ang), jnp.sin(ang)
    return jnp.concatenate([c, c], axis=-1), jnp.concatenate([-s, s], axis=-1)


def _retention_log_decay():
    return jnp.log1p(-jnp.exp2(-5.0 - jnp.arange(RET_HEADS, dtype=F32)))


def _ret_prompt_kernel(q_ref, k_ref, v_ref, dec_ref, xi_ref, zeta_ref, gc_ref, o_ref, st_ref, state):
    c = pl.program_id(0)

    @pl.when(c == 0)
    def _():
        state[...] = jnp.zeros_like(state)

    for hd in range(RET_HEADS):
        qk = slice(hd * RET_DK, (hd + 1) * RET_DK)
        vs = slice(hd * RET_DV, (hd + 1) * RET_DV)
        q = q_ref[:, qk].astype(BF16)
        k = k_ref[:, qk]
        v = v_ref[:, vs].astype(BF16)
        s_old = state[hd]
        scores = lax.dot_general(q, k.astype(BF16), (((1,), (1,)), ((), ())), preferred_element_type=F32)
        scores = scores * dec_ref[hd]
        o_inner = jnp.dot(scores.astype(BF16), v, preferred_element_type=F32)
        o_cross = jnp.dot(q, s_old.astype(BF16), preferred_element_type=F32) * xi_ref[hd]
        o_ref[:, vs] = o_inner + o_cross
        kz = (k * zeta_ref[hd]).astype(BF16)
        upd = lax.dot_general(kz, v, (((0,), (0,)), ((), ())), preferred_element_type=F32)
        state[hd] = gc_ref[hd] * s_old + upd

    @pl.when(c == pl.num_programs(0) - 1)
    def _():
        st_ref[...] = state[...]


def _ret_prompt(qr, kr, vr):
    L = qr.shape[0]
    C = _row_tile(L, RET_CHUNK)
    log_g = _retention_log_decay()
    idx = jnp.arange(C, dtype=F32)
    diff = idx[:, None] - idx[None, :]
    causal = diff >= 0
    dec = jnp.where(causal[None], jnp.exp(jnp.where(causal, diff, 0.0)[None] * log_g[:, None, None]), 0.0)
    xi = jnp.exp((idx + 1.0)[None, :, None] * log_g[:, None, None])
    zeta = jnp.exp((C - 1.0 - idx)[None, :, None] * log_g[:, None, None])
    gc = jnp.broadcast_to(jnp.exp(C * log_g)[:, None, None], (RET_HEADS, 1, RET_DV))
    return pl.pallas_call(
        _ret_prompt_kernel,
        grid=(L // C,),
        in_specs=[
            pl.BlockSpec((C, RET_Q_W), lambda c: (c, 0)),
            pl.BlockSpec((C, RET_Q_W), lambda c: (c, 0)),
            pl.BlockSpec((C, RET_V_W), lambda c: (c, 0)),
            pl.BlockSpec((RET_HEADS, C, C), lambda c: (0, 0, 0)),
            pl.BlockSpec((RET_HEADS, C, 1), lambda c: (0, 0, 0)),
            pl.BlockSpec((RET_HEADS, C, 1), lambda c: (0, 0, 0)),
            pl.BlockSpec((RET_HEADS, 1, RET_DV), lambda c: (0, 0, 0)),
        ],
        out_specs=[
            pl.BlockSpec((C, RET_V_W), lambda c: (c, 0)),
            pl.BlockSpec((RET_HEADS, RET_DK, RET_DV), lambda c: (0, 0, 0)),
        ],
        out_shape=[
            jax.ShapeDtypeStruct((L, RET_V_W), F32),
            jax.ShapeDtypeStruct((RET_HEADS, RET_DK, RET_DV), F32),
        ],
        scratch_shapes=[pltpu.VMEM((RET_HEADS, RET_DK, RET_DV), F32)],
        compiler_params=_params(("arbitrary",)),
        name="retention_prompt",
    )(qr, kr, vr, dec, xi, zeta, gc)


def _ret_sample_kernel(gam_ref, qt_ref, kt_ref, v_ref, s_ref, o_ref, sn_ref):
    bb = v_ref.shape[0]
    for i in range(bb):
        for h in range(RET_HEADS):
            gam = gam_ref[h]
            qc = qt_ref[h, :, i:i + 1]
            kc = kt_ref[h, :, i:i + 1]
            vrow = v_ref[i, h:h + 1, :]
            st = s_ref[i, h]
            qk = jnp.sum(qc * kc, axis=0, keepdims=True)
            cross = jnp.sum(qc * st, axis=0, keepdims=True) * gam
            o_ref[i, h:h + 1, :] = qk * vrow + cross
            sn_ref[i, h] = gam * st + kc * vrow


def _ret_sample(qr, kr, vr, state):
    db = qr.shape[0]
    bb = _row_tile(db, SAMPLE_BATCH_BLOCK)
    nb = db // bb

    def cols(t):
        return t.reshape(nb, bb, RET_HEADS, RET_DK).transpose(0, 2, 3, 1)

    gam = jnp.exp(_retention_log_decay())
    o, s_new = pl.pallas_call(
        _ret_sample_kernel,
        grid_spec=pltpu.PrefetchScalarGridSpec(
            num_scalar_prefetch=1,
            grid=(nb,),
            in_specs=[
                pl.BlockSpec((None, RET_HEADS, RET_DK, bb), lambda b, g: (b, 0, 0, 0)),
                pl.BlockSpec((None, RET_HEADS, RET_DK, bb), lambda b, g: (b, 0, 0, 0)),
                pl.BlockSpec((bb, RET_HEADS, RET_DV), lambda b, g: (b, 0, 0)),
                pl.BlockSpec((bb, RET_HEADS, RET_DK, RET_DV), lambda b, g: (b, 0, 0, 0)),
            ],
            out_specs=[
                pl.BlockSpec((bb, RET_HEADS, RET_DV), lambda b, g: (b, 0, 0)),
                pl.BlockSpec((bb, RET_HEADS, RET_DK, RET_DV), lambda b, g: (b, 0, 0, 0)),
            ],
        ),
        out_shape=[
            jax.ShapeDtypeStruct((db, RET_HEADS, RET_DV), F32),
            jax.ShapeDtypeStruct((db, RET_HEADS, RET_DK, RET_DV), F32),
        ],
        compiler_params=_params(("parallel",)),
        name="retention_sample",
    )(gam, cols(qr), cols(kr), vr.reshape(db, RET_HEADS, RET_DV), state)
    return o.reshape(db, RET_V_W), s_new


def _softmax_sink_pv(s, sink, v_bf):
    m = jnp.maximum(jnp.max(s, axis=-1, keepdims=True), sink)
    p = jnp.exp(s - m)
    denom = jnp.sum(p, axis=-1, keepdims=True) + jnp.exp(sink - m)
    pv = jnp.dot(p.astype(BF16), v_bf, preferred_element_type=F32)
    return pv / denom


def _swa_prompt_kernel(sink_ref, q_ref, kp_ref, kc_ref, vp_ref, vc_ref, o_ref):
    b = pl.program_id(0)
    blk = q_ref.shape[0]
    qi = lax.broadcasted_iota(jnp.int32, (blk, 2 * blk), 0)
    kj = lax.broadcasted_iota(jnp.int32, (blk, 2 * blk), 1)
    valid = (kj > qi + (blk - WINDOW)) & (kj <= qi + blk) & ((kj >= blk) | (b > 0))
    k2 = jnp.concatenate([kp_ref[...], kc_ref[...]], axis=0).astype(BF16)
    v2 = jnp.concatenate([vp_ref[...], vc_ref[...]], axis=0).astype(BF16)
    for hh in range(ATT_HEADS):
        j = hh // ATT_GROUP
        qh = q_ref[:, hh * ATT_HEAD_DIM:(hh + 1) * ATT_HEAD_DIM].astype(BF16)
        kh = k2[:, j * ATT_HEAD_DIM:(j + 1) * ATT_HEAD_DIM]
        vh = v2[:, j * ATT_HEAD_DIM:(j + 1) * ATT_HEAD_DIM]
        s = lax.dot_general(qh, kh, (((1,), (1,)), ((), ())), preferred_element_type=F32)
        s = jnp.where(valid, s * (ATT_HEAD_DIM ** -0.5), -jnp.inf)
        o_ref[:, hh * ATT_HEAD_DIM:(hh + 1) * ATT_HEAD_DIM] = _softmax_sink_pv(s, sink_ref[hh], vh)


def _swa_prompt(sinks, qa, ka, va):
    L = qa.shape[0]
    blk = SWA_BLOCK
    assert L % blk == 0 and blk >= WINDOW
    cur = lambda w: pl.BlockSpec((blk, w), lambda b, s: (b, 0))
    prev = lambda w: pl.BlockSpec((blk, w), lambda b, s: (jnp.maximum(b - 1, 0), 0))
    return pl.pallas_call(
        _swa_prompt_kernel,
        grid_spec=pltpu.PrefetchScalarGridSpec(
            num_scalar_prefetch=1,
            grid=(L // blk,),
            in_specs=[cur(ATT_Q_W), prev(ATT_KV_W), cur(ATT_KV_W), prev(ATT_KV_W), cur(ATT_KV_W)],
            out_specs=cur(ATT_Q_W),
        ),
        out_shape=jax.ShapeDtypeStruct((L, ATT_Q_W), F32),
        compiler_params=_params(("parallel",)),
        name="swa_prompt",
    )(sinks, qa, ka, ka, va, va)


def _swa_sample_kernel(sink_ref, q_ref, kn_ref, vn_ref, kb_ref, vb_ref, o_ref, ko_ref, vo_ref):
    bb, nbuf = kb_ref.shape[0], kb_ref.shape[1]
    for i in range(bb):
        ko_ref[i, 0:nbuf - 1, :] = kb_ref[i, 1:nbuf, :]
        ko_ref[i, nbuf - 1:nbuf, :] = kn_ref[i:i + 1, :]
        vo_ref[i, 0:nbuf - 1, :] = vb_ref[i, 1:nbuf, :]
        vo_ref[i, nbuf - 1:nbuf, :] = vn_ref[i:i + 1, :]
        kk = ko_ref[i].astype(BF16)
        vv = vo_ref[i].astype(BF16)
        q = q_ref[i].astype(BF16)
        for j in range(ATT_KV_HEADS):
            qj = q[j * ATT_GROUP:(j + 1) * ATT_GROUP]
            kj = kk[:, j * ATT_HEAD_DIM:(j + 1) * ATT_HEAD_DIM]
            vj = vv[:, j * ATT_HEAD_DIM:(j + 1) * ATT_HEAD_DIM]
            s = lax.dot_general(qj, kj, (((1,), (1,)), ((), ())), preferred_element_type=F32)
            s = s * (ATT_HEAD_DIM ** -0.5)
            sink = sink_ref[j * ATT_GROUP:(j + 1) * ATT_GROUP, :]
            o_ref[i, j * ATT_GROUP:(j + 1) * ATT_GROUP, :] = _softmax_sink_pv(s, sink, vj)


def _swa_sample(sinks, qa, ka, va, k_buf, v_buf):
    db, nbuf = k_buf.shape[0], k_buf.shape[1]
    assert nbuf == WINDOW, "single-token step with a full window-sized cache"
    bb = _row_tile(db, SAMPLE_BATCH_BLOCK)
    kb = k_buf.reshape(db, nbuf, ATT_KV_W)
    vb = v_buf.reshape(db, nbuf, ATT_KV_W)
    row = lambda w: pl.BlockSpec((bb, w), lambda b: (b, 0))
    cache = pl.BlockSpec((bb, nbuf, ATT_KV_W), lambda b: (b, 0, 0))
    heads = pl.BlockSpec((bb, ATT_HEADS, ATT_HEAD_DIM), lambda b: (b, 0, 0))
    o, ko, vo = pl.pallas_call(
        _swa_sample_kernel,
        grid=(db // bb,),
        in_specs=[pl.BlockSpec((ATT_HEADS, 1), lambda b: (0, 0)), heads, row(ATT_KV_W), row(ATT_KV_W), cache, cache],
        out_specs=[heads, cache, cache],
        out_shape=[
            jax.ShapeDtypeStruct((db, ATT_HEADS, ATT_HEAD_DIM), F32),
            jax.ShapeDtypeStruct((db, nbuf, ATT_KV_W), F32),
            jax.ShapeDtypeStruct((db, nbuf, ATT_KV_W), F32),
        ],
        compiler_params=_params(("parallel",)),
        name="swa_sample",
    )(sinks.reshape(ATT_HEADS, 1), qa.reshape(db, ATT_HEADS, ATT_HEAD_DIM), ka, va, kb, vb)
    shape5 = (db, nbuf, ATT_KV_HEADS, ATT_HEAD_DIM)
    return o.reshape(db, ATT_Q_W), ko.reshape(shape5), vo.reshape(shape5)


def _layer_norm(z, w, b):
    mu = jnp.mean(z, axis=-1, keepdims=True)
    zc = z - mu
    var = jnp.mean(zc * zc, axis=-1, keepdims=True)
    return zc * lax.rsqrt(var + LN_EPS) * w + b


def _mix_kernel(or_ref, g_ref, oa_ref, x_ref, wo_ref, gnw_ref, gnb_ref, lnw_ref, lnb_ref, wrh_ref, wrl_ref, br_ref,
                h_ref, h3_ref, idx_ref, gate_ref, mixed):
    for hd in range(RET_HEADS):
        sl = slice(hd * RET_DV, (hd + 1) * RET_DV)
        o = or_ref[:, sl]
        mu = jnp.mean(o, axis=-1, keepdims=True)
        oc = o - mu
        var = jnp.mean(oc * oc, axis=-1, keepdims=True)
        on = oc * lax.rsqrt(var + GN_EPS) * gnw_ref[:, sl] + gnb_ref[:, sl]
        g = g_ref[:, sl]
        mixed[:, sl] = (g * jax.nn.sigmoid(g) * on).astype(BF16)
    mixed[:, RET_V_W:] = oa_ref[...].astype(BF16)
    acc = jnp.dot(mixed[...], wo_ref[...], preferred_element_type=F32)
    h = _layer_norm(DEEPNORM_ALPHA * x_ref[...] + acc, lnw_ref[...], lnb_ref[...])
    h_ref[...] = h
    h3_ref[...] = h.reshape(h3_ref.shape)

    h_hi = h.astype(BF16)
    h_lo = (h - h_hi.astype(F32)).astype(BF16)
    logits = (jnp.dot(h_hi, wrh_ref[...], preferred_element_type=F32)
              + (jnp.dot(h_hi, wrl_ref[...], preferred_element_type=F32)
                 + jnp.dot(h_lo, wrh_ref[...], preferred_element_type=F32))) + br_ref[...]
    lane = lax.broadcasted_iota(jnp.int32, logits.shape, 1)
    vals, idxs = [], []
    for _ in range(TOP_K):
        m = jnp.max(logits, axis=-1, keepdims=True)
        sel = jnp.min(jnp.where(logits == m, lane, N_EXPERTS), axis=-1, keepdims=True)
        vals.append(m)
        idxs.append(sel)
        logits = jnp.where(lane == sel, -jnp.inf, logits)
    exps = [jnp.exp(v - vals[0]) for v in vals]
    tot = exps[0] + exps[1] + exps[2] + exps[3]
    for kk in range(TOP_K):
        idx_ref[:, kk:kk + 1] = idxs[kk]
        gate_ref[:, kk:kk + 1] = exps[kk] / tot


def _mix(o_r, g_r, o_a, x, wo_bf, gn_w, gn_b, ln_w, ln_b, wr_hi, wr_lo, b_router):
    n = x.shape[0]
    tm = _row_tile(n, PROJ_ROWS)
    row = lambda w: pl.BlockSpec((tm, w), lambda i: (i, 0))
    full = lambda a: pl.BlockSpec(a.shape, lambda i: (0,) * a.ndim)
    return pl.pallas_call(
        _mix_kernel,
        grid=(n // tm,),
        in_specs=[row(RET_V_W), row(RET_V_W), row(ATT_Q_W), row(D_MODEL), full(wo_bf), full(gn_w), full(gn_b),
                  full(ln_w), full(ln_b), full(wr_hi), full(wr_lo), full(b_router)],
        out_specs=[row(D_MODEL), pl.BlockSpec((tm, 1, D_MODEL), lambda i: (i, 0, 0)), row(TOP_K), row(TOP_K)],
        out_shape=[
            jax.ShapeDtypeStruct((n, D_MODEL), F32),
            jax.ShapeDtypeStruct((n, 1, D_MODEL), F32),
            jax.ShapeDtypeStruct((n, TOP_K), jnp.int32),
            jax.ShapeDtypeStruct((n, TOP_K), F32),
        ],
        scratch_shapes=[pltpu.VMEM((tm, RET_V_W + ATT_Q_W), BF16)],
        compiler_params=_params(("parallel",)),
        name="mix_outproj_ln1_router",
    )(o_r, g_r, o_a, x, wo_bf, gn_w, gn_b, ln_w, ln_b, wr_hi, wr_lo, b_router)


def _routing_common(top_idx, tm, n_prompt):
    nk = top_idx.size
    flat_e = top_idx.reshape(nk)
    onehot = (flat_e[:, None] == jnp.arange(N_EXPERTS, dtype=jnp.int32)[None, :]).astype(jnp.int32)
    csum = jnp.cumsum(onehot, axis=0)
    rank = jnp.sum(csum * onehot, axis=1) - 1
    counts = csum[-1]
    items = (counts + tm - 1) // tm
    item_end = jnp.cumsum(items)
    item_start = item_end - items
    pos = ((item_start * tm)[flat_e] + rank).astype(jnp.int32)
    pair_bits = (nk - 1).bit_length()
    assert N_EXPERTS << pair_bits < 2 ** 31, "expert and pair index share one int32 sort key"
    keys = jnp.sort((flat_e << pair_bits) | jnp.arange(nk, dtype=jnp.int32))
    order = keys & ((1 << pair_bits) - 1)
    starts = jnp.cumsum(counts) - counts
    prompt_counts = csum[n_prompt * TOP_K - 1]
    return dict(pos=pos, order=order, counts=counts, item_end=item_end, item_start=item_start,
                starts=starts, prompt_counts=prompt_counts, n_used=item_end[-1].astype(jnp.int32))


def _item_tables(rt, nk, tm, n_items):
    counts, item_end, item_start, starts = rt["counts"], rt["item_end"], rt["item_start"], rt["starts"]
    prompt_counts, n_used = rt["prompt_counts"], rt["n_used"]
    w_all = jnp.arange(n_items, dtype=jnp.int32)
    w = jnp.minimum(w_all, n_used - 1)
    item_e = jnp.minimum(jnp.sum((item_end[None, :] <= w[:, None]).astype(jnp.int32), axis=1), N_EXPERTS - 1)
    item_rows = jnp.where(w_all < n_used, jnp.clip(counts[item_e] - (w - item_start[item_e]) * tm, 0, tm), 0)
    subs = tm // MOE_SUB
    sub_off = jnp.arange(subs, dtype=jnp.int32)[None, :] * MOE_SUB
    blk_local = ((w - item_start[item_e]) * tm)[:, None] + sub_off
    blk_nv = jnp.clip(item_rows[:, None] - sub_off, 0, MOE_SUB)
    blk_np = jnp.clip(prompt_counts[item_e][:, None] - blk_local, 0, blk_nv)
    blk_base = jnp.clip(starts[item_e][:, None] + blk_local, 0, nk - 1)
    flat = lambda a: a.reshape(n_items * subs).astype(jnp.int32)
    return (item_e.astype(jnp.int32), item_rows.astype(jnp.int32), flat(blk_base), flat(blk_np), flat(blk_nv))


def _dispatch_kernel(order_ref, base_ref, np_ref, nv_ref, hp_ref, hs_ref, xg_ref, buf, sem, *, n_prompt):
    b = pl.program_id(0)
    base = base_ref[b]

    @pl.when(b == 0)
    def _():
        buf[...] = jnp.zeros_like(buf)

    n_p = np_ref[b]
    n_v = nv_ref[b]

    def token(r):
        return lax.shift_right_logical(order_ref[base + r], TOP_K.bit_length() - 1)

    def copy_p(r):
        return pltpu.make_async_copy(hp_ref.at[token(r)], buf.at[pl.ds(r, 1), :], sem)

    def copy_s(r):
        return pltpu.make_async_copy(hs_ref.at[token(r) - n_prompt], buf.at[pl.ds(r, 1), :], sem)

    def sweep(wait):
        def go(c):
            if wait:
                c.wait()
            else:
                c.start()

        def group(g, carry):
            for u in range(DMA_UNROLL):
                go(copy_p(g * DMA_UNROLL + u))
            return carry

        def one_p(r, carry):
            go(copy_p(r))
            return carry

        def one_s(r, carry):
            go(copy_s(r))
            return carry

        n_g = n_p // DMA_UNROLL
        lax.fori_loop(0, n_g, group, 0)
        lax.fori_loop(n_g * DMA_UNROLL, n_p, one_p, 0)
        lax.fori_loop(n_p, n_v, one_s, 0)

    sweep(False)
    sweep(True)
    xg_ref[...] = buf[...].astype(BF16)


def _dispatch(order, blk_base, blk_np, blk_nv, h3_p, h3_s):
    assert TOP_K & (TOP_K - 1) == 0, "pair index -> token uses a shift"
    n_blocks = blk_np.shape[0]
    return pl.pallas_call(
        functools.partial(_dispatch_kernel, n_prompt=h3_p.shape[0]),
        grid_spec=pltpu.PrefetchScalarGridSpec(
            num_scalar_prefetch=4,
            grid=(n_blocks,),
            in_specs=[pl.BlockSpec(memory_space=pl.ANY)] * 2,
            out_specs=pl.BlockSpec((MOE_SUB, D_MODEL), lambda b, *_: (b, 0)),
            scratch_shapes=[pltpu.VMEM((MOE_SUB, D_MODEL), F32), pltpu.SemaphoreType.DMA(())],
        ),
        out_shape=jax.ShapeDtypeStruct((n_blocks * MOE_SUB, D_MODEL), BF16),
        compiler_params=_params(("arbitrary",)),
        name="moe_dispatch",
    )(order, blk_base, blk_np, blk_nv, h3_p, h3_s)


def _expert_kernel(item_e_ref, n_used_ref, rows_ref, x_ref, w1g_ref, w1l_ref, b1g_ref, b1l_ref, w2_ref, b2_ref,
                   o_ref, act, *, nj):
    w = pl.program_id(0)
    s = pl.program_id(1)
    used = w < n_used_ref[0]
    n_sub = (rows_ref[w] + MOE_SUB - 1) // MOE_SUB

    def phase1(k):
        wg = w1g_ref[...].astype(BF16)
        wl = w1l_ref[...].astype(BF16)
        for sb in range(k):
            rows = slice(sb * MOE_SUB, (sb + 1) * MOE_SUB)
            x = x_ref[rows, :]
            hg = jnp.dot(x, wg, preferred_element_type=F32) + b1g_ref[...]
            hl = jnp.dot(x, wl, preferred_element_type=F32) + b1l_ref[...]
            glu = jnp.minimum(hg, SWIGLU_LIMIT)
            lin = jnp.clip(hl, -SWIGLU_LIMIT, SWIGLU_LIMIT)
            act[s, rows, :] = (glu * jax.nn.sigmoid(SWIGLU_ALPHA * glu) * (lin + 1.0)).astype(BF16)

    def phase2(k):
        w2b = w2_ref[...].astype(BF16)
        for sb in range(MOE_SUBS_PER_ITEM):
            rows = slice(sb * MOE_SUB, (sb + 1) * MOE_SUB)
            if sb < k:
                a = jnp.concatenate([act[jj, rows, :] for jj in range(nj)], axis=1)
                res = jnp.dot(a, w2b, preferred_element_type=F32) + b2_ref[...]
                o_ref[rows, :] = res
            else:
                o_ref[rows, :] = jnp.zeros((MOE_SUB, o_ref.shape[1]), F32)

    for k in range(1, MOE_SUBS_PER_ITEM + 1):
        pl.when(used & (s < nj) & (n_sub == k))(functools.partial(phase1, k))
        pl.when(used & (s >= nj) & (n_sub == k))(functools.partial(phase2, k))

    @pl.when(jnp.logical_not(used) & (s >= nj))
    def _():
        o_ref[...] = jnp.zeros_like(o_ref)


def _experts(item_e, n_used, item_rows, xg, w1, b1, w2, b2, n_items, tm):
    tf = MOE_FF
    nj = D_FF // tf
    nn = D_MODEL // tf

    def item(w, nu):
        return jnp.minimum(w, nu[0] - 1)

    def ff(w, s, nu):
        return jnp.where(w < nu[0], jnp.minimum(s, nj - 1), nj - 1)

    def col(w, s, nu):
        return jnp.where(w < nu[0], jnp.maximum(s - nj, 0), nn - 1)

    b1r = b1.reshape(N_EXPERTS, 1, 2 * D_FF)
    b2r = b2.reshape(N_EXPERTS, 1, D_MODEL)
    return pl.pallas_call(
        functools.partial(_expert_kernel, nj=nj),
        grid_spec=pltpu.PrefetchScalarGridSpec(
            num_scalar_prefetch=3,
            grid=(n_items, nj + nn),
            in_specs=[
                pl.BlockSpec((tm, D_MODEL), lambda w, s, ie, nu, nr: (item(w, nu), 0)),
                pl.BlockSpec((None, D_MODEL, tf), lambda w, s, ie, nu, nr: (ie[w], 0, ff(w, s, nu))),
                pl.BlockSpec((None, D_MODEL, tf), lambda w, s, ie, nu, nr: (ie[w], 0, nj + ff(w, s, nu))),
                pl.BlockSpec((None, 1, tf), lambda w, s, ie, nu, nr: (ie[w], 0, ff(w, s, nu))),
                pl.BlockSpec((None, 1, tf), lambda w, s, ie, nu, nr: (ie[w], 0, nj + ff(w, s, nu))),
                pl.BlockSpec((None, D_FF, tf), lambda w, s, ie, nu, nr: (ie[w], 0, col(w, s, nu))),
                pl.BlockSpec((None, 1, tf), lambda w, s, ie, nu, nr: (ie[w], 0, col(w, s, nu))),
            ],
            out_specs=pl.BlockSpec((tm, tf), lambda w, s, ie, nu, nr: (w, jnp.maximum(s - nj, 0))),
            scratch_shapes=[pltpu.VMEM((nj, tm, tf), BF16)],
        ),
        out_shape=jax.ShapeDtypeStruct((n_items * tm, D_MODEL), F32),
        compiler_params=_params(("arbitrary", "arbitrary")),
        name="moe_experts",
    )(item_e, n_used, item_rows, xg, w1, w1, b1r, b1r, w2, b2r)


def _combine_kernel(pos_ref, gate_ref, h_ref, eo_ref, lnw_ref, lnb_ref, y_ref, buf, sem, *, tc):
    base = pl.program_id(0) * (tc * TOP_K)

    tokens_per_group = DMA_UNROLL // TOP_K

    def mk(g, u):
        t = g * tokens_per_group + u // TOP_K
        k = u % TOP_K
        return pltpu.make_async_copy(eo_ref.at[pl.ds(pos_ref[base + g * DMA_UNROLL + u], 1), :],
                                     buf.at[pl.ds(k * tc + t, 1), :], sem)

    def start(g, carry):
        for u in range(DMA_UNROLL):
            mk(g, u).start()
        return carry

    def wait(g, carry):
        for u in range(DMA_UNROLL):
            mk(g, u).wait()
        return carry

    lax.fori_loop(0, tc * TOP_K // DMA_UNROLL, start, 0)
    lax.fori_loop(0, tc * TOP_K // DMA_UNROLL, wait, 0)
    gates = gate_ref[...]
    f = jnp.zeros((tc, D_MODEL), F32)
    for k in range(TOP_K):
        f = f + gates[:, k:k + 1] * buf[k * tc:(k + 1) * tc, :]
    y_ref[...] = _layer_norm(DEEPNORM_ALPHA * h_ref[...] + f, lnw_ref[...], lnb_ref[...])


def _combine(pos, gates, h, eo3, ln_w, ln_b):
    n = h.shape[0]
    tc = _row_tile(n, COMBINE_ROWS)
    assert (tc * TOP_K) % DMA_UNROLL == 0 and DMA_UNROLL % TOP_K == 0
    full = lambda a: pl.BlockSpec(a.shape, lambda i, p: (0,) * a.ndim)
    return pl.pallas_call(
        functools.partial(_combine_kernel, tc=tc),
        grid_spec=pltpu.PrefetchScalarGridSpec(
            num_scalar_prefetch=1,
            grid=(n // tc,),
            in_specs=[
                pl.BlockSpec((tc, TOP_K), lambda i, p: (i, 0)),
                pl.BlockSpec((tc, D_MODEL), lambda i, p: (i, 0)),
                pl.BlockSpec(memory_space=pl.ANY),
                full(ln_w), full(ln_b),
            ],
            out_specs=pl.BlockSpec((tc, D_MODEL), lambda i, p: (i, 0)),
            scratch_shapes=[pltpu.VMEM((tc * TOP_K, D_MODEL), F32), pltpu.SemaphoreType.DMA(())],
        ),
        out_shape=jax.ShapeDtypeStruct((n, D_MODEL), F32),
        compiler_params=_params(("arbitrary",)),
        name="moe_combine_ln2",
    )(pos, gates, h, eo3, ln_w, ln_b)


def _row2(a):
    return a.reshape(1, -1)


def kernel(x_prompt, x_sample, state_ret, cache_swa_k, cache_swa_v, w_in, b_in, sinks, gn_w, gn_b, w_out,
           ln1_w, ln1_b, w_router, b_router, w1, b1, w2, b2, ln2_w, ln2_b):
    B, L, _ = x_prompt.shape
    DB, S, _ = x_sample.shape
    assert B == 1 and S == 1 and w_in.shape[0] == DEPTH == 1
    xp = x_prompt.reshape(L, D_MODEL)
    xs = x_sample.reshape(DB, D_MODEL)
    w_in_bf = w_in[0].astype(BF16)
    w_out_bf = w_out[0].astype(BF16)
    b_in2 = _row2(b_in[0])

    cos_p, sin_p = _rope_tables(jnp.arange(L, dtype=F32))
    qr, kr, vr, g_r, qa, ka, va = _in_proj(xp, w_in_bf, b_in2, cos_p, sin_p)
    o_r, st_p = _ret_prompt(qr, kr, vr)
    o_a = _swa_prompt(sinks[0], qa, ka, va)
    rows = min(WINDOW, L)
    k_new_p = ka[L - rows:].reshape(1, 1, rows, ATT_KV_HEADS, ATT_HEAD_DIM)
    v_new_p = va[L - rows:].reshape(1, 1, rows, ATT_KV_HEADS, ATT_HEAD_DIM)

    cos_s, sin_s = _rope_tables(jnp.full((DB,), PAST_LEN, dtype=F32))
    qr_s, kr_s, vr_s, g_s, qa_s, ka_s, va_s = _in_proj(xs, w_in_bf, b_in2, cos_s, sin_s)
    o_r_s, st_s = _ret_sample(qr_s, kr_s, vr_s, state_ret[0])
    o_a_s, k_new_s, v_new_s = _swa_sample(sinks[0], qa_s, ka_s, va_s, cache_swa_k[0], cache_swa_v[0])

    wr_hi = w_router[0].astype(BF16)
    wr_lo = (w_router[0] - wr_hi.astype(F32)).astype(BF16)
    mix_w = (w_out_bf, _row2(gn_w[0]), _row2(gn_b[0]), _row2(ln1_w[0]), _row2(ln1_b[0]), wr_hi, wr_lo,
             _row2(b_router[0]))
    h_p, h3_p, idx_p, gate_p = _mix(o_r, g_r, o_a, xp, *mix_w)
    h_s, h3_s, idx_s, gate_s = _mix(o_r_s, g_s, o_a_s, xs, *mix_w)

    top_idx = jnp.concatenate([idx_p, idx_s], axis=0)
    nk = top_idx.size
    rt = _routing_common(top_idx, MOE_ROWS, L)
    ln2 = (_row2(ln2_w[0]), _row2(ln2_b[0]))

    def moe(n_items, rt, h3_p, h3_s, h_p, h_s, gate_p, gate_s, w1e, b1e, w2e, b2e, ln_w, ln_b):
        item_e, item_rows, blk_base, blk_np, blk_nv = _item_tables(rt, nk, MOE_ROWS, n_items)
        xg = _dispatch(rt["order"], blk_base, blk_np, blk_nv, h3_p, h3_s)
        eo = _experts(item_e, rt["n_used"].reshape(1), item_rows, xg, w1e, b1e, w2e, b2e, n_items, MOE_ROWS)
        return (_combine(rt["pos"][:L * TOP_K], gate_p, h_p, eo, ln_w, ln_b),
                _combine(rt["pos"][L * TOP_K:], gate_s, h_s, eo, ln_w, ln_b))

    max_items = nk // MOE_ROWS + N_EXPERTS
    small_items = min(max_items, N_EXPERTS + MOE_SPARE_ITEMS)
    operands = (rt, h3_p, h3_s, h_p, h_s, gate_p, gate_s, w1[0], b1[0], w2[0], b2[0], *ln2)
    y_p, y_s = lax.cond(rt["n_used"] <= small_items,
                        functools.partial(moe, small_items), functools.partial(moe, max_items), *operands)

    return (y_p.reshape(1, L, D_MODEL), y_s.reshape(DB, 1, D_MODEL),
            st_p.reshape(1, 1, RET_HEADS, RET_DK, RET_DV), k_new_p, v_new_p,
            st_s[None], k_new_s[None], v_new_s[None])
```
